```python
import jax
import jax.numpy as jnp
from jax import lax
import numpy as np

D_MODEL = 1024
BATCH = 8
SEQ = 4096
DEPTH = 1

MIX_WIDTH = D_MODEL
DN_HEAD_DIM = 128
DN_WIDTH = MIX_WIDTH // 2
DN_HEADS = DN_WIDTH // DN_HEAD_DIM
DN_CONV = 4
DN_CHUNK = 64
SB_HEAD_DIM = 64
SB_WIDTH = MIX_WIDTH - DN_WIDTH
SB_HEADS = SB_WIDTH // SB_HEAD_DIM
SB_BLOCK = 128
IN_SIZES = (DN_WIDTH, DN_WIDTH, DN_WIDTH, DN_WIDTH, DN_HEADS, DN_HEADS, SB_WIDTH, SB_WIDTH, SB_WIDTH)
IN_COLS = sum(IN_SIZES)
PEER_HEADS = 8
PEER_N_KEYS = 128
PEER_EXPERTS = PEER_N_KEYS * PEER_N_KEYS
PEER_QUERY_DIM = 256
PEER_HALF = PEER_QUERY_DIM // 2
PEER_TOPK = 16
PEER_TOKEN_BLOCK = 128
N_MOD = 6
EPS = 1e-6

kernel_name = "hybrid_deltanet_stickbreak_peer_block"


def rms_norm(x, gain):
    xf = x.astype(jnp.float32)
    y = xf * lax.rsqrt(jnp.mean(xf * xf, axis=-1, keepdims=True) + EPS)
    return (y * gain.astype(jnp.float32)).astype(x.dtype)


def l2_norm(x):
    xf = x.astype(jnp.float32)
    return xf * lax.rsqrt(jnp.sum(xf * xf, axis=-1, keepdims=True) + EPS)


def causal_depthwise_conv(u, w):
    ch = u.shape[-1]
    return lax.conv_general_dilated(
        u, w[:, None, :].astype(u.dtype), window_strides=(1,), padding=[(w.shape[0] - 1, 0)],
        dimension_numbers=("NWC", "WIO", "NWC"), feature_group_count=ch)


def gated_delta_rule(q, k, v, g, beta):
    b, h, s, dk = q.shape
    dv = v.shape[-1]
    c = DN_CHUNK
    n = s // c
    q = q.reshape(b, h, n, c, dk)
    k = k.reshape(b, h, n, c, dk)
    v = v.reshape(b, h, n, c, dv)
    g = jnp.cumsum(g.reshape(b, h, n, c), axis=-1)
    beta = beta.reshape(b, h, n, c)[..., None]
    idx = jnp.arange(c)
    lower_incl = idx[:, None] >= idx[None, :]
    lower_strict = idx[:, None] > idx[None, :]
    decay = jnp.exp(jnp.where(lower_incl, g[..., :, None] - g[..., None, :], -jnp.inf))
    kbeta = k * beta
    kk = jnp.einsum('bhnid,bhnjd->bhnij', kbeta, k)
    tri = jnp.eye(c, dtype=jnp.float32) + jnp.where(lower_strict, kk * decay, 0.0)
    rhs = jnp.concatenate([kbeta * jnp.exp(g)[..., None], v * beta], axis=-1)
    sol = lax.linalg.triangular_solve(tri, rhs, left_side=True, lower=True, unit_diagonal=True)
    w_c, u_c = sol[..., :dk], sol[..., dk:]
    attn = jnp.einsum('bhnid,bhnjd->bhnij', q, k) * decay
    q_dec = q * jnp.exp(g)[..., None]
    k_dec = k * jnp.exp(g[..., -1:] - g)[..., None]
    g_last = jnp.exp(g[..., -1])

    def step(state, xs):
        q_i, k_i, w_i, u_i, a_i, gl_i = xs
        v_new = u_i - jnp.einsum('bhcd,bhde->bhce', w_i, state)
        o_i = jnp.einsum('bhcd,bhde->bhce', q_i, state) + jnp.einsum('bhij,bhje->bhie', a_i, v_new)
        state = state * gl_i[..., None, None] + jnp.einsum('bhcd,bhce->bhde', k_i, v_new)
        return state, o_i

    xs = tuple(jnp.moveaxis(t, 2, 0) for t in (q_dec, k_dec, w_c, u_c, attn, g_last))
    state0 = jnp.zeros((b, h, dk, dv), jnp.float32)
    _, o = lax.scan(step, state0, xs)
    return jnp.moveaxis(o, 0, 2).reshape(b, h, s, dv)


def stick_breaking_attention(q, k, v):
    s_len, d = q.shape[2], q.shape[3]
    scale = d ** -0.5
    outs = []
    for i in range(s_len // SB_BLOCK):
        t0 = i * SB_BLOCK
        t1 = t0 + SB_BLOCK
        z = jnp.einsum('bhtd,bhsd->bhts', q[:, :, t0:t1], k[:, :, :t1]) * scale
        causal = jnp.arange(t1)[None, :] < jnp.arange(t0, t1)[:, None]
        log_beta = jax.nn.log_sigmoid(z)
        log_1m = jnp.where(causal, jax.nn.log_sigmoid(-z), 0.0)
        after = lax.cumsum(log_1m, axis=3, reverse=True) - log_1m
        a = jnp.where(causal, jnp.exp(log_beta + after), 0.0)
        outs.append(jnp.einsum('bhts,bhsd->bhtd', a, v[:, :, :t1]))
    return jnp.concatenate(outs, axis=2)


def hybrid_mixer(h, w_in, conv_w, a_log, dt_bias, dn_out_gain, sb_q_gain, sb_k_gain, sb_out_gain, w_out):
    b, s, _ = h.shape
    f32 = jnp.float32
    proj = h @ w_in
    offs = [int(o) for o in np.cumsum(IN_SIZES)[:-1]]
    dq, dk, dv, dz, db, da, sq, sk, sv = jnp.split(proj, offs, axis=-1)
    qkv = jax.nn.silu(causal_depthwise_conv(jnp.concatenate([dq, dk, dv], axis=-1), conv_w))
    dq, dk, dv = jnp.split(qkv, 3, axis=-1)
    q = l2_norm(dq.reshape(b, s, DN_HEADS, DN_HEAD_DIM)) * DN_HEAD_DIM ** -0.5
    k = l2_norm(dk.reshape(b, s, DN_HEADS, DN_HEAD_DIM))
    v = dv.reshape(b, s, DN_HEADS, DN_HEAD_DIM).astype(f32)
    beta = jax.nn.sigmoid(db.astype(f32))
    g = -jnp.exp(a_log.astype(f32)) * jax.nn.softplus(da.astype(f32) + dt_bias.astype(f32))
    o = gated_delta_rule(q.transpose(0, 2, 1, 3), k.transpose(0, 2, 1, 3), v.transpose(0, 2, 1, 3),
                         g.transpose(0, 2, 1), beta.transpose(0, 2, 1)).transpose(0, 2, 1, 3)
    o_dn = rms_norm(o, dn_out_gain) * jax.nn.silu(dz.reshape(b, s, DN_HEADS, DN_HEAD_DIM).astype(f32))
    o_dn = o_dn.reshape(b, s, DN_WIDTH)
    qs = rms_norm(sq.reshape(b, s, SB_HEADS, SB_HEAD_DIM), sb_q_gain).astype(f32).transpose(0, 2, 1, 3)
    ks = rms_norm(sk.reshape(b, s, SB_HEADS, SB_HEAD_DIM), sb_k_gain).astype(f32).transpose(0, 2, 1, 3)
    vs = sv.reshape(b, s, SB_HEADS, SB_HEAD_DIM).astype(f32).transpose(0, 2, 1, 3)
    o_sb = stick_breaking_attention(qs, ks, vs).transpose(0, 2, 1, 3)
    o_sb = rms_norm(o_sb, sb_out_gain).reshape(b, s, SB_WIDTH)
    mixed = jnp.concatenate([o_dn, o_sb], axis=-1).astype(h.dtype)
    return mixed @ w_out


def peer_ffn(h, w_query, sub_keys, w_u, w_v):
    b, s, d = h.shape
    hb = h.reshape(b * s // PEER_TOKEN_BLOCK, PEER_TOKEN_BLOCK, d)

    def block(ht):
        t = ht.shape[0]
        q = (ht @ w_query).reshape(t, PEER_HEADS, 2, PEER_HALF)
        scores = jnp.einsum('thpk,hpnk->thpn', q, sub_keys)
        sv, si = lax.top_k(scores, PEER_TOPK)
        cand_s = sv[:, :, 0, :, None] + sv[:, :, 1, None, :]
        cand_i = si[:, :, 0, :, None] * PEER_N_KEYS + si[:, :, 1, None, :]
        top_s, top_j = lax.top_k(cand_s.reshape(t, PEER_HEADS, PEER_TOPK * PEER_TOPK), PEER_TOPK)
        expert = jnp.take_along_axis(cand_i.reshape(t, PEER_HEADS, PEER_TOPK * PEER_TOPK), top_j, axis=-1)
        gate = jax.nn.softmax(top_s.astype(jnp.float32), axis=-1)
        pre = jnp.einsum('td,thkd->thk', ht, w_u[expert]).astype(jnp.float32)
        act = jax.nn.gelu(pre, approximate=False) * gate
        return jnp.einsum('thk,thkd->td', act.astype(ht.dtype), w_v[expert])

    return lax.map(block, hb).reshape(b, s, d)


def setup_inputs(seed: int = 0) -> dict:
    key = jax.random.key(seed)
    ks = jax.random.split(key, 20)
    f32 = jnp.float32
    d = D_MODEL
    dt = jnp.exp(jax.random.uniform(ks[6], (DEPTH, DN_HEADS), f32, np.log(1e-3), np.log(1e-1)))
    return {
        "x": jax.random.normal(ks[0], (BATCH, SEQ, d), f32),
        "c": jax.random.normal(ks[1], (BATCH, d), f32),
        "w_ada": jax.random.normal(ks[2], (DEPTH, d, N_MOD * d), f32) * (0.5 * d ** -0.5),
        "b_ada": jax.random.normal(ks[3], (DEPTH, N_MOD * d), f32) * 0.02,
        "norm1_gain": 1.0 + 0.02 * jax.random.normal(ks[4], (DEPTH, d), f32),
        "w_in": jax.random.normal(ks[5], (DEPTH, d, IN_COLS), f32) * d ** -0.5,
        "dn_conv_w": jax.random.normal(ks[7], (DEPTH, DN_CONV, 3 * DN_WIDTH), f32) * DN_CONV ** -0.5,
        "dn_a_log": jnp.log(jax.random.uniform(ks[8], (DEPTH, DN_HEADS), f32, 1.0, 16.0)),
        "dn_dt_bias": dt + jnp.log(-jnp.expm1(-dt)),
        "dn_out_gain": 1.0 + 0.02 * jax.random.normal(ks[9], (DEPTH, DN_HEAD_DIM), f32),
        "sb_q_gain": 1.0 + 0.02 * jax.random.normal(ks[10], (DEPTH, SB_HEAD_DIM), f32),
        "sb_k_gain": 1.0 + 0.02 * jax.random.normal(ks[11], (DEPTH, SB_HEAD_DIM), f32),
        "sb_out_gain": 1.0 + 0.02 * jax.random.normal(ks[12], (DEPTH, SB_HEAD_DIM), f32),
        "w_out": jax.random.normal(ks[13], (DEPTH, MIX_WIDTH, d), f32) * MIX_WIDTH ** -0.5,
        "norm2_gain": 1.0 + 0.02 * jax.random.normal(ks[14], (DEPTH, d), f32),
        "peer_w_query": jax.random.normal(ks[15], (DEPTH, d, PEER_HEADS * PEER_QUERY_DIM), f32) * d ** -0.5,
        "peer_sub_keys": jax.random.normal(ks[16], (DEPTH, PEER_HEADS, 2, PEER_N_KEYS, PEER_HALF), f32) * PEER_HALF ** -0.5,
        "peer_w_u": jax.random.normal(ks[17], (DEPTH, PEER_EXPERTS, d), f32) * d ** -0.5,
        "peer_w_v": jax.random.normal(ks[18], (DEPTH, PEER_EXPERTS, d), f32),
    }


def reference(x, c, w_ada, b_ada, norm1_gain, w_in, dn_conv_w, dn_a_log, dn_dt_bias, dn_out_gain,
              sb_q_gain, sb_k_gain, sb_out_gain, w_out, norm2_gain, peer_w_query, peer_sub_keys,
              peer_w_u, peer_w_v):
    for l in range(DEPTH):
        mod = jax.nn.silu(c) @ w_ada[l] + b_ada[l]
        shift1, scale1, gate1, shift2, scale2, gate2 = jnp.split(mod[:, None, :], N_MOD, axis=-1)
        h = rms_norm(x, norm1_gain[l]) * (1.0 + scale1) + shift1
        mixed = hybrid_mixer(h, w_in[l], dn_conv_w[l], dn_a_log[l], dn_dt_bias[l], dn_out_gain[l],
                             sb_q_gain[l], sb_k_gain[l], sb_out_gain[l], w_out[l])
        x = x + gate1 * mixed
        h = rms_norm(x, norm2_gain[l]) * (1.0 + scale2) + shift2
        x = x + gate2 * peer_ffn(h, peer_w_query[l], peer_sub_keys[l], peer_w_u[l], peer_w_v[l])
    return x
```

```python
import functools

import numpy as np
import jax
import jax.numpy as jnp
from jax import lax
from jax.experimental import pallas as pl
from jax.experimental.pallas import tpu as pltpu

F32 = jnp.float32
BF16 = jnp.bfloat16
HIGHEST = lax.Precision.HIGHEST

D_MODEL = 1024
DN_HEAD_DIM = 128
DN_WIDTH = 512
DN_HEADS = 4
DN_CONV = 4
DN_CHUNK = 64
SB_HEAD_DIM = 64
SB_WIDTH = 512
SB_HEADS = 8
PEER_HEADS = 8
PEER_N_KEYS = 128
PEER_HALF = 128
PEER_TOPK = 16
PEER_TOKEN_BLOCK = 128
N_MOD = 6
EPS = 1e-6

LANES = 128
VMEM_LIMIT = 56 * 1024 * 1024

QKV_COLS = 3 * DN_WIDTH
Z_OFF = QKV_COLS
SB_OFF = Z_OFF + DN_WIDTH
BG_OFF = SB_OFF + 3 * SB_WIDTH
IN_COLS_PADDED = BG_OFF + LANES

TS_PROJ = 256
SB_TILE = 256


def _params(*sem):
    return pltpu.CompilerParams(dimension_semantics=sem, vmem_limit_bytes=VMEM_LIMIT)


def _ada_kernel(c_ref, w_ref, b_ref, o_ref):
    c = c_ref[...]
    sc = c * jax.nn.sigmoid(c)
    o_ref[...] = jnp.dot(sc, w_ref[...], precision=HIGHEST, preferred_element_type=F32) + b_ref[...]


def _ada_mod(c, w_ada, b_ada):
    b, d = c.shape
    n = w_ada.shape[1]
    tn = 512
    return pl.pallas_call(
        _ada_kernel,
        grid=(n // tn,),
        in_specs=[pl.BlockSpec((b, d), lambda j: (0, 0)),
                  pl.BlockSpec((d, tn), lambda j: (0, j)),
                  pl.BlockSpec((1, tn), lambda j: (0, j))],
        out_specs=pl.BlockSpec((b, tn), lambda j: (0, j)),
        out_shape=jax.ShapeDtypeStruct((b, n), F32),
        compiler_params=_params("parallel"),
        name="ada_mod",
    )(c, w_ada, b_ada.reshape(1, n))


def _modulated_norm(x, gain, scale, shift):
    y = x * lax.rsqrt(jnp.mean(x * x, axis=-1, keepdims=True) + EPS)
    return (y * gain) * (1.0 + scale) + shift


def _inproj_kernel(x_ref, g_ref, sc_ref, sh_ref, w_ref, qkv_ref, z_ref, sb_ref, bg_ref):
    h = _modulated_norm(x_ref[0], g_ref[...], sc_ref[0], sh_ref[0]).astype(BF16)
    qkv_ref[0] = jnp.dot(h, w_ref[:, 0:Z_OFF], preferred_element_type=F32)
    z_ref[0] = jnp.dot(h, w_ref[:, Z_OFF:SB_OFF], preferred_element_type=F32)
    sb_ref[0] = jnp.dot(h, w_ref[:, SB_OFF:BG_OFF], preferred_element_type=F32)
    bg_ref[0] = jnp.dot(h, w_ref[:, BG_OFF:IN_COLS_PADDED], preferred_element_type=F32)


def _in_projection(x, gain, scale, shift, w):
    b, s, d = x.shape
    ts = TS_PROJ
    tok = lambda n: pl.BlockSpec((1, ts, n), lambda i, j: (i, j, 0))
    per_batch = pl.BlockSpec((1, 1, d), lambda i, j: (i, 0, 0))
    return pl.pallas_call(
        _inproj_kernel,
        grid=(b, s // ts),
        in_specs=[tok(d), pl.BlockSpec((1, d), lambda i, j: (0, 0)), per_batch, per_batch,
                  pl.BlockSpec((d, IN_COLS_PADDED), lambda i, j: (0, 0))],
        out_specs=[tok(QKV_COLS), tok(DN_WIDTH), tok(3 * SB_WIDTH), tok(LANES)],
        out_shape=[jax.ShapeDtypeStruct((b, s, QKV_COLS), F32),
                   jax.ShapeDtypeStruct((b, s, DN_WIDTH), F32),
                   jax.ShapeDtypeStruct((b, s, 3 * SB_WIDTH), F32),
                   jax.ShapeDtypeStruct((b, s, LANES), F32)],
        compiler_params=_params("parallel", "parallel"),
        name="in_projection",
    )(x, gain.reshape(1, d), scale, shift, w)


def _head_mean_square(x, blockdiag):
    xx = x * x
    hi = xx.astype(BF16)
    lo = (xx - hi.astype(F32)).astype(BF16)
    ss = jnp.dot(hi, blockdiag, preferred_element_type=F32) + jnp.dot(lo, blockdiag, preferred_element_type=F32)
    return ss * (1.0 / SB_HEAD_DIM)


def _sb_prep_kernel(sb_ref, bd_ref, gq_ref, gk_ref, q_ref, k_ref, v_ref):
    bd = bd_ref[...]
    q = sb_ref[0, :, 0:SB_WIDTH]
    k = sb_ref[0, :, SB_WIDTH:2 * SB_WIDTH]
    qn = q * lax.rsqrt(_head_mean_square(q, bd) + EPS) * gq_ref[...]
    kn = k * lax.rsqrt(_head_mean_square(k, bd) + EPS) * gk_ref[...]
    q_ref[0] = (qn * SB_HEAD_DIM ** -0.5).astype(BF16)
    k_ref[0] = kn.astype(BF16)
    v_ref[0] = sb_ref[0, :, 2 * SB_WIDTH:3 * SB_WIDTH].astype(BF16)


def _sb_prep(sb, q_gain, k_gain):
    b, s, _ = sb.shape
    ts = TS_PROJ
    head_of = np.arange(SB_WIDTH) // SB_HEAD_DIM
    blockdiag = jnp.asarray(head_of[:, None] == head_of[None, :], BF16)
    tok = lambda n: pl.BlockSpec((1, ts, n), lambda i, j: (i, j, 0))
    const = lambda r, c: pl.BlockSpec((r, c), lambda i, j: (0, 0))
    out = jax.ShapeDtypeStruct((b, s, SB_WIDTH), BF16)
    return pl.pallas_call(
        _sb_prep_kernel,
        grid=(b, s // ts),
        in_specs=[tok(3 * SB_WIDTH), const(SB_WIDTH, SB_WIDTH), const(1, SB_WIDTH), const(1, SB_WIDTH)],
        out_specs=[tok(SB_WIDTH)] * 3,
        out_shape=[out] * 3,
        compiler_params=_params("parallel", "parallel"),
        name="sb_prep",
    )(sb, blockdiag, jnp.tile(q_gain, SB_HEADS).reshape(1, SB_WIDTH), jnp.tile(k_gain, SB_HEADS).reshape(1, SB_WIDTH))


HEADS_PER_STEP = LANES // SB_HEAD_DIM


def _sb_kernel(q_ref, k_ref, v_ref, u_ref, g_ref, o_ref, acc_ref, carry_ref):
    qi = pl.program_id(2)
    j = pl.program_id(3)
    t = SB_TILE

    @pl.when(j == 0)
    def _():
        acc_ref[...] = jnp.zeros_like(acc_ref)
        carry_ref[...] = jnp.zeros_like(carry_ref)

    def step(diagonal):
        for h in range(HEADS_PER_STEP):
            lanes = slice(h * SB_HEAD_DIM, (h + 1) * SB_HEAD_DIM)
            q = q_ref[0, :, lanes]
            k = k_ref[0, :, lanes]
            v = v_ref[0, :, lanes]
            z = lax.dot_general(q, k, (((1,), (1,)), ((), ())), preferred_element_type=F32)
            log_beta = jnp.minimum(z, 0.0) - jnp.log1p(jnp.exp(-jnp.abs(z)))
            log_1m = log_beta - z
            if diagonal:
                causal = lax.broadcasted_iota(jnp.int32, (t, t), 1) < lax.broadcasted_iota(jnp.int32, (t, t), 0)
                log_1m = jnp.where(causal, log_1m, 0.0)
            hi = log_1m.astype(BF16)
            lo = (log_1m - hi.astype(F32)).astype(BF16)
            u = u_ref[...]
            after = jnp.dot(hi, u, preferred_element_type=F32) + jnp.dot(lo, u, preferred_element_type=F32)
            carry = carry_ref[h]
            a = jnp.exp(log_beta + after + carry)
            if diagonal:
                a = jnp.where(causal, a, 0.0)
            acc_ref[h] += jnp.dot(a.astype(BF16), v, preferred_element_type=F32)
            carry_ref[h] = carry + jnp.sum(log_1m, axis=-1, keepdims=True)

    @pl.when(j == 0)
    def _():
        step(True)

    @pl.when(jnp.logical_and(j > 0, j <= qi))
    def _():
        step(False)

    @pl.when(j == qi)
    def _():
        for h in range(HEADS_PER_STEP):
            o = acc_ref[h]
            o = o * lax.rsqrt(jnp.mean(o * o, axis=-1, keepdims=True) + EPS) * g_ref[...]
            o_ref[0, :, h * SB_HEAD_DIM:(h + 1) * SB_HEAD_DIM] = o


def _sb_attention(qn, kn, vb, out_gain):
    b, s, _ = qn.shape
    t = SB_TILE
    n = s // t
    idx = np.arange(t)
    suffix = jnp.asarray(idx[:, None] > idx[None, :], BF16)
    qspec = pl.BlockSpec((1, t, LANES), lambda bi, hp, qi, j: (bi, qi, hp))
    kspec = pl.BlockSpec((1, t, LANES), lambda bi, hp, qi, j: (bi, jnp.maximum(qi - j, 0), hp))
    return pl.pallas_call(
        _sb_kernel,
        grid=(b, SB_WIDTH // LANES, n, n),
        in_specs=[qspec, kspec, kspec,
                  pl.BlockSpec((t, t), lambda bi, hp, qi, j: (0, 0)),
                  pl.BlockSpec((1, SB_HEAD_DIM), lambda bi, hp, qi, j: (0, 0))],
        out_specs=qspec,
        out_shape=jax.ShapeDtypeStruct((b, s, SB_WIDTH), F32),
        scratch_shapes=[pltpu.VMEM((HEADS_PER_STEP, t, SB_HEAD_DIM), F32),
                        pltpu.VMEM((HEADS_PER_STEP, t, 1), F32)],
        compiler_params=_params("parallel", "parallel", "arbitrary", "arbitrary"),
        name="sb_attention",
    )(qn, kn, vb, suffix, out_gain.reshape(1, SB_HEAD_DIM))


def _outproj_kernel(odn_ref, osb_ref, x_ref, g1_ref, w_ref, n2_ref, sc2_ref, sh2_ref, x1_ref, h2_ref):
    mixed = (jnp.dot(odn_ref[0].astype(BF16), w_ref[0:DN_WIDTH, :], preferred_element_type=F32)
             + jnp.dot(osb_ref[0].astype(BF16), w_ref[DN_WIDTH:DN_WIDTH + SB_WIDTH, :], preferred_element_type=F32))
    x1 = x_ref[0] + g1_ref[0] * mixed
    x1_ref[0] = x1
    h2_ref[0] = _modulated_norm(x1, n2_ref[...], sc2_ref[0], sh2_ref[0])


def _out_projection(o_dn, o_sb, x, gate1, w_out, gain2, scale2, shift2):
    b, s, d = x.shape
    ts = TS_PROJ
    tok = lambda n: pl.BlockSpec((1, ts, n), lambda i, j: (i, j, 0))
    per_batch = pl.BlockSpec((1, 1, d), lambda i, j: (i, 0, 0))
    out = jax.ShapeDtypeStruct((b, s, d), F32)
    return pl.pallas_call(
        _outproj_kernel,
        grid=(b, s // ts),
        in_specs=[tok(DN_WIDTH), tok(SB_WIDTH), tok(d), per_batch,
                  pl.BlockSpec((d, d), lambda i, j: (0, 0)),
                  pl.BlockSpec((1, d), lambda i, j: (0, 0)), per_batch, per_batch],
        out_specs=[tok(d), tok(d)],
        out_shape=[out, out],
        compiler_params=_params("parallel", "parallel"),
        name="out_projection",
    )(o_dn, o_sb, x, gate1, w_out, gain2.reshape(1, d), scale2, shift2)


def _rms_norm(x, gain):
    return x * lax.rsqrt(jnp.mean(x * x, axis=-1, keepdims=True) + EPS) * gain


def _l2_norm(x):
    return x * lax.rsqrt(jnp.sum(x * x, axis=-1, keepdims=True) + EPS)


def _gated_delta_rule(q, k, v, g, beta):
    b, h, s, dk = q.shape
    dv = v.shape[-1]
    c = DN_CHUNK
    n = s // c
    q = q.reshape(b, h, n, c, dk)
    k = k.reshape(b, h, n, c, dk)
    v = v.reshape(b, h, n, c, dv)
    g = jnp.cumsum(g.reshape(b, h, n, c), axis=-1)
    beta = beta.reshape(b, h, n, c)[..., None]
    idx = jnp.arange(c)
    lower_incl = idx[:, None] >= idx[None, :]
    lower_strict = idx[:, None] > idx[None, :]
    decay = jnp.exp(jnp.where(lower_incl, g[..., :, None] - g[..., None, :], -jnp.inf))
    kbeta = k * beta
    kk = jnp.einsum('bhnid,bhnjd->bhnij', kbeta, k)
    tri = jnp.eye(c, dtype=F32) + jnp.where(lower_strict, kk * decay, 0.0)
    rhs = jnp.concatenate([kbeta * jnp.exp(g)[..., None], v * beta], axis=-1)
    sol = lax.linalg.triangular_solve(tri, rhs, left_side=True, lower=True, unit_diagonal=True)
    w_c, u_c = sol[..., :dk], sol[..., dk:]
    attn = jnp.einsum('bhnid,bhnjd->bhnij', q, k) * decay
    q_dec = q * jnp.exp(g)[..., None]
    k_dec = k * jnp.exp(g[..., -1:] - g)[..., None]
    g_last = jnp.exp(g[..., -1])

    def step(state, xs):
        q_i, k_i, w_i, u_i, a_i, gl_i = xs
        v_new = u_i - jnp.einsum('bhcd,bhde->bhce', w_i, state)
        o_i = jnp.einsum('bhcd,bhde->bhce', q_i, state) + jnp.einsum('bhij,bhje->bhie', a_i, v_new)
        state = state * gl_i[..., None, None] + jnp.einsum('bhcd,bhce->bhde', k_i, v_new)
        return state, o_i

    xs = tuple(jnp.moveaxis(t, 2, 0) for t in (q_dec, k_dec, w_c, u_c, attn, g_last))
    _, o = lax.scan(step, jnp.zeros((b, h, dk, dv), F32), xs)
    return jnp.moveaxis(o, 0, 2).reshape(b, h, s, dv)


def _deltanet_jax(qkv, dz, bg, conv_w, a_log, dt_bias, out_gain):
    b, s, _ = qkv.shape
    conv = lax.conv_general_dilated(
        qkv, conv_w[:, None, :], window_strides=(1,), padding=[(DN_CONV - 1, 0)],
        dimension_numbers=("NWC", "WIO", "NWC"), feature_group_count=qkv.shape[-1])
    dq, dk, dv = jnp.split(jax.nn.silu(conv), 3, axis=-1)
    q = _l2_norm(dq.reshape(b, s, DN_HEADS, DN_HEAD_DIM)) * DN_HEAD_DIM ** -0.5
    k = _l2_norm(dk.reshape(b, s, DN_HEADS, DN_HEAD_DIM))
    v = dv.reshape(b, s, DN_HEADS, DN_HEAD_DIM)
    beta = jax.nn.sigmoid(bg[..., 0:DN_HEADS])
    g = -jnp.exp(a_log) * jax.nn.softplus(bg[..., DN_HEADS:2 * DN_HEADS] + dt_bias)
    o = _gated_delta_rule(q.transpose(0, 2, 1, 3), k.transpose(0, 2, 1, 3), v.transpose(0, 2, 1, 3),
                          g.transpose(0, 2, 1), beta.transpose(0, 2, 1)).transpose(0, 2, 1, 3)
    o = _rms_norm(o, out_gain) * jax.nn.silu(dz.reshape(b, s, DN_HEADS, DN_HEAD_DIM))
    return o.reshape(b, s, DN_WIDTH)


def _peer_jax(h, w_query, sub_keys, w_u, w_v):
    b, s, d = h.shape
    hb = h.reshape(b * s // PEER_TOKEN_BLOCK, PEER_TOKEN_BLOCK, d)

    def block(ht):
        t = ht.shape[0]
        q = (ht @ w_query).reshape(t, PEER_HEADS, 2, PEER_HALF)
        scores = jnp.einsum('thpk,hpnk->thpn', q, sub_keys)
        sv, si = lax.top_k(scores, PEER_TOPK)
        cand_s = sv[:, :, 0, :, None] + sv[:, :, 1, None, :]
        cand_i = si[:, :, 0, :, None] * PEER_N_KEYS + si[:, :, 1, None, :]
        top_s, top_j = lax.top_k(cand_s.reshape(t, PEER_HEADS, PEER_TOPK * PEER_TOPK), PEER_TOPK)
        expert = jnp.take_along_axis(cand_i.reshape(t, PEER_HEADS, PEER_TOPK * PEER_TOPK), top_j, axis=-1)
        gate = jax.nn.softmax(top_s, axis=-1)
        pre = jnp.einsum('td,thkd->thk', ht, w_u[expert])
        act = jax.nn.gelu(pre, approximate=False) * gate
        return jnp.einsum('thk,thkd->td', act, w_v[expert])

    return lax.map(block, hb).reshape(b, s, d)


def _regroup_w_in(w_in):
    bg0 = 4 * DN_WIDTH
    sb0 = bg0 + 2 * DN_HEADS
    pad = jnp.zeros((w_in.shape[0], LANES - 2 * DN_HEADS), w_in.dtype)
    return jnp.concatenate([w_in[:, :bg0], w_in[:, sb0:], w_in[:, bg0:sb0], pad], axis=1).astype(BF16)


def kernel(x, c, w_ada, b_ada, norm1_gain, w_in, dn_conv_w, dn_a_log, dn_dt_bias, dn_out_gain, sb_q_gain,
           sb_k_gain, sb_out_gain, w_out, norm2_gain, peer_w_query, peer_sub_keys, peer_w_u, peer_w_v):
    for l in range(w_ada.shape[0]):
        mod = _ada_mod(c, w_ada[l], b_ada[l])
        shift1, scale1, gate1, shift2, scale2, gate2 = jnp.split(mod[:, None, :], N_MOD, axis=-1)
        qkv, dz, sb, bg = _in_projection(x, norm1_gain[l], scale1, shift1, _regroup_w_in(w_in[l]))
        o_dn = _deltanet_jax(qkv, dz, bg, dn_conv_w[l], dn_a_log[l], dn_dt_bias[l], dn_out_gain[l])
        qn, kn, vb = _sb_prep(sb, sb_q_gain[l], sb_k_gain[l])
        o_sb = _sb_attention(qn, kn, vb, sb_out_gain[l])
        x1, h2 = _out_projection(o_dn, o_sb, x, gate1, w_out[l].astype(BF16), norm2_gain[l], scale2, shift2)
        x = x1 + gate2 * _peer_jax(h2, peer_w_query[l], peer_sub_keys[l], peer_w_u[l], peer_w_v[l])
    return x
```

```python
import functools

import numpy as np
import jax
import jax.numpy as jnp
from jax import lax
from jax.experimental import pallas as pl
from jax.experimental.pallas import tpu as pltpu

F32 = jnp.float32
BF16 = jnp.bfloat16
HIGHEST = lax.Precision.HIGHEST

D_MODEL = 1024
DN_HEAD_DIM = 128
DN_WIDTH = 512
DN_HEADS = 4
DN_CONV = 4
DN_CHUNK = 64
SB_HEAD_DIM = 64
SB_WIDTH = 512
SB_HEADS = 8
PEER_HEADS = 8
PEER_N_KEYS = 128
PEER_HALF = 128
PEER_TOPK = 16
PEER_TOKEN_BLOCK = 128
N_MOD = 6
EPS = 1e-6

LANES = 128
VMEM_LIMIT = 56 * 1024 * 1024

QKV_COLS = 3 * DN_WIDTH
Z_OFF = QKV_COLS
SB_OFF = Z_OFF + DN_WIDTH
BG_OFF = SB_OFF + 3 * SB_WIDTH
IN_COLS_PADDED = BG_OFF + LANES

TS_PROJ = 256
SB_TILE = 256


def _params(*sem):
    return pltpu.CompilerParams(dimension_semantics=sem, vmem_limit_bytes=VMEM_LIMIT)


def _ada_kernel(c_ref, w_ref, b_ref, o_ref):
    c = c_ref[...]
    sc = c * jax.nn.sigmoid(c)
    o_ref[...] = jnp.dot(sc, w_ref[...], precision=HIGHEST, preferred_element_type=F32) + b_ref[...]


def _ada_mod(c, w_ada, b_ada):
    b, d = c.shape
    n = w_ada.shape[1]
    tn = 512
    return pl.pallas_call(
        _ada_kernel,
        grid=(n // tn,),
        in_specs=[pl.BlockSpec((b, d), lambda j: (0, 0)),
                  pl.BlockSpec((d, tn), lambda j: (0, j)),
                  pl.BlockSpec((1, tn), lambda j: (0, j))],
        out_specs=pl.BlockSpec((b, tn), lambda j: (0, j)),
        out_shape=jax.ShapeDtypeStruct((b, n), F32),
        compiler_params=_params("parallel"),
        name="ada_mod",
    )(c, w_ada, b_ada.reshape(1, n))


def _modulated_norm(x, gain, scale, shift):
    y = x * lax.rsqrt(jnp.mean(x * x, axis=-1, keepdims=True) + EPS)
    return (y * gain) * (1.0 + scale) + shift


def _inproj_kernel(x_ref, g_ref, sc_ref, sh_ref, w_ref, qkv_ref, z_ref, sb_ref, bg_ref):
    h = _modulated_norm(x_ref[0], g_ref[...], sc_ref[0], sh_ref[0]).astype(BF16)
    qkv_ref[0] = jnp.dot(h, w_ref[:, 0:Z_OFF], preferred_element_type=F32)
    z_ref[0] = jnp.dot(h, w_ref[:, Z_OFF:SB_OFF], preferred_element_type=F32)
    sb_ref[0] = jnp.dot(h, w_ref[:, SB_OFF:BG_OFF], preferred_element_type=F32)
    bg_ref[0] = jnp.dot(h, w_ref[:, BG_OFF:IN_COLS_PADDED], preferred_element_type=F32)


def _in_projection(x, gain, scale, shift, w):
    b, s, d = x.shape
    ts = TS_PROJ
    tok = lambda n: pl.BlockSpec((1, ts, n), lambda i, j: (i, j, 0))
    per_batch = pl.BlockSpec((1, 1, d), lambda i, j: (i, 0, 0))
    return pl.pallas_call(
        _inproj_kernel,
        grid=(b, s // ts),
        in_specs=[tok(d), pl.BlockSpec((1, d), lambda i, j: (0, 0)), per_batch, per_batch,
                  pl.BlockSpec((d, IN_COLS_PADDED), lambda i, j: (0, 0))],
        out_specs=[tok(QKV_COLS), tok(DN_WIDTH), tok(3 * SB_WIDTH), tok(LANES)],
        out_shape=[jax.ShapeDtypeStruct((b, s, QKV_COLS), F32),
                   jax.ShapeDtypeStruct((b, s, DN_WIDTH), F32),
                   jax.ShapeDtypeStruct((b, s, 3 * SB_WIDTH), F32),
                   jax.ShapeDtypeStruct((b, s, LANES), F32)],
        compiler_params=_params("parallel", "parallel"),
        name="in_projection",
    )(x, gain.reshape(1, d), scale, shift, w)


def _head_mean_square(x, blockdiag):
    xx = x * x
    hi = xx.astype(BF16)
    lo = (xx - hi.astype(F32)).astype(BF16)
    ss = jnp.dot(hi, blockdiag, preferred_element_type=F32) + jnp.dot(lo, blockdiag, preferred_element_type=F32)
    return ss * (1.0 / SB_HEAD_DIM)


def _sb_prep_kernel(sb_ref, bd_ref, gq_ref, gk_ref, q_ref, k_ref, v_ref):
    bd = bd_ref[...]
    q = sb_ref[0, :, 0:SB_WIDTH]
    k = sb_ref[0, :, SB_WIDTH:2 * SB_WIDTH]
    qn = q * lax.rsqrt(_head_mean_square(q, bd) + EPS) * gq_ref[...]
    kn = k * lax.rsqrt(_head_mean_square(k, bd) + EPS) * gk_ref[...]
    q_ref[0] = (qn * SB_HEAD_DIM ** -0.5).astype(BF16)
    k_ref[0] = kn.astype(BF16)
    v_ref[0] = sb_ref[0, :, 2 * SB_WIDTH:3 * SB_WIDTH].astype(BF16)


def _sb_prep(sb, q_gain, k_gain):
    b, s, _ = sb.shape
    ts = TS_PROJ
    head_of = np.arange(SB_WIDTH) // SB_HEAD_DIM
    blockdiag = jnp.asarray(head_of[:, None] == head_of[None, :], BF16)
    tok = lambda n: pl.BlockSpec((1, ts, n), lambda i, j: (i, j, 0))
    const = lambda r, c: pl.BlockSpec((r, c), lambda i, j: (0, 0))
    out = jax.ShapeDtypeStruct((b, s, SB_WIDTH), BF16)
    return pl.pallas_call(
        _sb_prep_kernel,
        grid=(b, s // ts),
        in_specs=[tok(3 * SB_WIDTH), const(SB_WIDTH, SB_WIDTH), const(1, SB_WIDTH), const(1, SB_WIDTH)],
        out_specs=[tok(SB_WIDTH)] * 3,
        out_shape=[out] * 3,
        compiler_params=_params("parallel", "parallel"),
        name="sb_prep",
    )(sb, blockdiag, jnp.tile(q_gain, SB_HEADS).reshape(1, SB_WIDTH), jnp.tile(k_gain, SB_HEADS).reshape(1, SB_WIDTH))


HEADS_PER_STEP = LANES // SB_HEAD_DIM


def _sb_kernel(q_ref, k_ref, v_ref, u_ref, g_ref, o_ref, acc_ref, carry_ref):
    qi = pl.program_id(2)
    j = pl.program_id(3)
    t = SB_TILE

    @pl.when(j == 0)
    def _():
        acc_ref[...] = jnp.zeros_like(acc_ref)
        carry_ref[...] = jnp.zeros_like(carry_ref)

    def step(diagonal):
        for h in range(HEADS_PER_STEP):
            lanes = slice(h * SB_HEAD_DIM, (h + 1) * SB_HEAD_DIM)
            q = q_ref[0, :, lanes]
            k = k_ref[0, :, lanes]
            v = v_ref[0, :, lanes]
            z = lax.dot_general(q, k, (((1,), (1,)), ((), ())), preferred_element_type=F32)
            log_beta = jnp.minimum(z, 0.0) - jnp.log1p(jnp.exp(-jnp.abs(z)))
            log_1m = log_beta - z
            if diagonal:
                causal = lax.broadcasted_iota(jnp.int32, (t, t), 1) < lax.broadcasted_iota(jnp.int32, (t, t), 0)
                log_1m = jnp.where(causal, log_1m, 0.0)
            hi = log_1m.astype(BF16)
            lo = (log_1m - hi.astype(F32)).astype(BF16)
            u = u_ref[...]
            after = jnp.dot(hi, u, preferred_element_type=F32) + jnp.dot(lo, u, preferred_element_type=F32)
            carry = carry_ref[h]
            a = jnp.exp(log_beta + after + carry)
            if diagonal:
                a = jnp.where(causal, a, 0.0)
            acc_ref[h] += jnp.dot(a.astype(BF16), v, preferred_element_type=F32)
            carry_ref[h] = carry + jnp.sum(log_1m, axis=-1, keepdims=True)

    @pl.when(j == 0)
    def _():
        step(True)

    @pl.when(jnp.logical_and(j > 0, j <= qi))
    def _():
        step(False)

    @pl.when(j == qi)
    def _():
        for h in range(HEADS_PER_STEP):
            o = acc_ref[h]
            o = o * lax.rsqrt(jnp.mean(o * o, axis=-1, keepdims=True) + EPS) * g_ref[...]
            o_ref[0, :, h * SB_HEAD_DIM:(h + 1) * SB_HEAD_DIM] = o


def _sb_attention(qn, kn, vb, out_gain):
    b, s, _ = qn.shape
    t = SB_TILE
    n = s // t
    idx = np.arange(t)
    suffix = jnp.asarray(idx[:, None] > idx[None, :], BF16)
    qspec = pl.BlockSpec((1, t, LANES), lambda bi, hp, qi, j: (bi, qi, hp))
    kspec = pl.BlockSpec((1, t, LANES), lambda bi, hp, qi, j: (bi, jnp.maximum(qi - j, 0), hp))
    return pl.pallas_call(
        _sb_kernel,
        grid=(b, SB_WIDTH // LANES, n, n),
        in_specs=[qspec, kspec, kspec,
                  pl.BlockSpec((t, t), lambda bi, hp, qi, j: (0, 0)),
                  pl.BlockSpec((1, SB_HEAD_DIM), lambda bi, hp, qi, j: (0, 0))],
        out_specs=qspec,
        out_shape=jax.ShapeDtypeStruct((b, s, SB_WIDTH), F32),
        scratch_shapes=[pltpu.VMEM((HEADS_PER_STEP, t, SB_HEAD_DIM), F32),
                        pltpu.VMEM((HEADS_PER_STEP, t, 1), F32)],
        compiler_params=_params("parallel", "parallel", "arbitrary", "arbitrary"),
        name="sb_attention",
    )(qn, kn, vb, suffix, out_gain.reshape(1, SB_HEAD_DIM))


def _outproj_kernel(odn_ref, osb_ref, x_ref, g1_ref, w_ref, n2_ref, sc2_ref, sh2_ref, x1_ref, h2_ref):
    mixed = (jnp.dot(odn_ref[0].astype(BF16), w_ref[0:DN_WIDTH, :], preferred_element_type=F32)
             + jnp.dot(osb_ref[0].astype(BF16), w_ref[DN_WIDTH:DN_WIDTH + SB_WIDTH, :], preferred_element_type=F32))
    x1 = x_ref[0] + g1_ref[0] * mixed
    x1_ref[0] = x1
    h2_ref[0] = _modulated_norm(x1, n2_ref[...], sc2_ref[0], sh2_ref[0])


def _out_projection(o_dn, o_sb, x, gate1, w_out, gain2, scale2, shift2):
    b, s, d = x.shape
    ts = TS_PROJ
    tok = lambda n: pl.BlockSpec((1, ts, n), lambda i, j: (i, j, 0))
    per_batch = pl.BlockSpec((1, 1, d), lambda i, j: (i, 0, 0))
    out = jax.ShapeDtypeStruct((b, s, d), F32)
    return pl.pallas_call(
        _outproj_kernel,
        grid=(b, s // ts),
        in_specs=[tok(DN_WIDTH), tok(SB_WIDTH), tok(d), per_batch,
                  pl.BlockSpec((d, d), lambda i, j: (0, 0)),
                  pl.BlockSpec((1, d), lambda i, j: (0, 0)), per_batch, per_batch],
        out_specs=[tok(d), tok(d)],
        out_shape=[out, out],
        compiler_params=_params("parallel", "parallel"),
        name="out_projection",
    )(o_dn, o_sb, x, gate1, w_out, gain2.reshape(1, d), scale2, shift2)


ROUTE_TOKENS = 256
N_CAND = PEER_TOPK * PEER_TOPK


def _top_k_rounds(scores, n_rounds, on_pick, carry):
    width = scores.shape[-1]
    lane = lax.broadcasted_iota(jnp.int32, scores.shape, 1).astype(F32)

    def body(i, state):
        s, c = state
        m = jnp.max(s, axis=-1, keepdims=True)
        pos = jnp.min(jnp.where(s == m, lane, float(width)), axis=-1, keepdims=True)
        c = on_pick(i, m, pos, c)
        return jnp.where(lane == pos, -jnp.inf, s), c

    return lax.fori_loop(0, n_rounds, body, (scores, carry))[1]


def _route_kernel(h_ref, wq_ref, keys_ref, e_ref, g_ref):
    h = h_ref[...].astype(BF16)
    t = h.shape[0]
    cand_lane = lax.broadcasted_iota(jnp.int32, (t, N_CAND), 1)
    cand_lane_f = cand_lane.astype(F32)
    cand_s = jnp.zeros((t, N_CAND), F32)
    cand_i = jnp.zeros((t, N_CAND), F32)
    for p in range(2):
        q = jnp.dot(h, wq_ref[:, p * PEER_HALF:(p + 1) * PEER_HALF], preferred_element_type=F32)
        s = lax.dot_general(q.astype(BF16), keys_ref[0, p], (((1,), (1,)), ((), ())), preferred_element_type=F32)
        rank_of = cand_lane // PEER_TOPK if p == 0 else cand_lane % PEER_TOPK
        idx_scale = PEER_N_KEYS if p == 0 else 1

        def on_pick(i, value, pos, c, rank_of=rank_of, idx_scale=idx_scale):
            cs, ci = c
            hit = rank_of == i
            return jnp.where(hit, cs + value, cs), jnp.where(hit, ci + pos * idx_scale, ci)

        cand_s, cand_i = _top_k_rounds(s, PEER_TOPK, on_pick, (cand_s, cand_i))

    out_lane = lax.broadcasted_iota(jnp.int32, (t, PEER_TOPK), 1)

    def on_pick2(i, value, pos, c):
        ts, te = c
        expert = jnp.max(jnp.where(cand_lane_f == pos, cand_i, -1.0), axis=-1, keepdims=True)
        hit = out_lane == i
        return jnp.where(hit, value, ts), jnp.where(hit, expert, te)

    top_s, top_e = _top_k_rounds(cand_s, PEER_TOPK, on_pick2,
                                 (jnp.zeros((t, PEER_TOPK), F32), jnp.zeros((t, PEER_TOPK), F32)))
    ex = jnp.exp(top_s - jnp.max(top_s, axis=-1, keepdims=True))
    g_ref[0] = ex / jnp.sum(ex, axis=-1, keepdims=True)
    e_ref[0] = top_e.astype(jnp.int32)


def _peer_route(h2, w_query, sub_keys):
    tkn, d = h2.shape
    tt = ROUTE_TOKENS
    out = lambda dt: jax.ShapeDtypeStruct((PEER_HEADS, tkn, PEER_TOPK), dt)
    ospec = pl.BlockSpec((1, tt, PEER_TOPK), lambda i, hd: (hd, i, 0))
    expert, gate = pl.pallas_call(
        _route_kernel,
        grid=(tkn // tt, PEER_HEADS),
        in_specs=[pl.BlockSpec((tt, d), lambda i, hd: (i, 0)),
                  pl.BlockSpec((d, 2 * PEER_HALF), lambda i, hd: (0, hd)),
                  pl.BlockSpec((1, 2, PEER_N_KEYS, PEER_HALF), lambda i, hd: (hd, 0, 0, 0))],
        out_specs=[ospec, ospec],
        out_shape=[out(jnp.int32), out(F32)],
        compiler_params=_params("parallel", "arbitrary"),
        name="peer_route",
    )(h2, w_query.astype(BF16), sub_keys.astype(BF16))
    flat = lambda a: a.transpose(1, 0, 2).reshape(tkn, PEER_HEADS * PEER_TOPK)
    return flat(expert), flat(gate)


APPLY_TOKENS = 8
ROWS_PER_TOKEN = PEER_HEADS * PEER_TOPK
APPLY_ROWS = APPLY_TOKENS * ROWS_PER_TOKEN


def _peer_apply_kernel(idx_ref, idx_next_ref, gate_ref, h_ref, x1_ref, g2_ref, eye_ref, tab_ref, o_ref,
                       buf_ref, sem_ref):
    i = pl.program_id(0)
    slot = i % 2
    d = h_ref.shape[-1]

    def issue(ids_ref, to_slot):
        def body(r, c):
            e = ids_ref[0, 0, r]
            pltpu.make_async_copy(tab_ref.at[pl.ds(e, 1)], buf_ref.at[to_slot, pl.ds(r, 1)],
                                  sem_ref.at[to_slot]).start()
            return c
        lax.fori_loop(0, APPLY_ROWS, body, 0, unroll=8)

    @pl.when(i == 0)
    def _():
        issue(idx_ref, 0)

    @pl.when(i + 1 < pl.num_programs(0))
    def _():
        issue(idx_next_ref, 1 - slot)

    pltpu.make_async_copy(tab_ref.at[pl.ds(0, APPLY_ROWS)], buf_ref.at[slot], sem_ref.at[slot]).wait()

    gate_t = lax.dot_general(eye_ref[...], gate_ref[...], (((1,), (1,)), ((), ())),
                             precision=HIGHEST, preferred_element_type=F32)
    tok_lane = lax.broadcasted_iota(jnp.int32, (ROWS_PER_TOKEN, APPLY_TOKENS), 1)
    pre = jnp.zeros((ROWS_PER_TOKEN, APPLY_TOKENS), F32)
    for t in range(APPLY_TOKENS):
        rows = pl.ds(t * ROWS_PER_TOKEN, ROWS_PER_TOKEN)
        u = buf_ref[slot, rows, 0:d]
        pre_t = jnp.sum(u * h_ref[t:t + 1, :], axis=-1, keepdims=True)
        pre = jnp.where(tok_lane == t, pre_t, pre)
    act = 0.5 * pre * (1.0 + lax.erf(pre * (2.0 ** -0.5))) * gate_t
    for t in range(APPLY_TOKENS):
        rows = pl.ds(t * ROWS_PER_TOKEN, ROWS_PER_TOKEN)
        v = buf_ref[slot, rows, d:2 * d]
        y = jnp.sum(act[:, t:t + 1] * v, axis=0, keepdims=True)
        o_ref[t:t + 1, :] = x1_ref[t:t + 1, :] + g2_ref[0] * y


def _peer_apply(expert, gate, h2, x1, gate2, table, tokens_per_batch):
    tkn, d = h2.shape
    tt = APPLY_TOKENS
    n = tkn // tt
    steps_per_batch = tokens_per_batch // tt
    ids = expert.reshape(n, 1, APPLY_ROWS)
    tok = lambda w: pl.BlockSpec((tt, w), lambda i: (i, 0))
    smem = lambda imap: pl.BlockSpec((1, 1, APPLY_ROWS), imap, memory_space=pltpu.SMEM)
    return pl.pallas_call(
        _peer_apply_kernel,
        grid=(n,),
        in_specs=[smem(lambda i: (i, 0, 0)),
                  smem(lambda i: (jnp.minimum(i + 1, n - 1), 0, 0)),
                  tok(ROWS_PER_TOKEN), tok(d), tok(d),
                  pl.BlockSpec((1, 1, d), lambda i: (i // steps_per_batch, 0, 0)),
                  pl.BlockSpec((ROWS_PER_TOKEN, ROWS_PER_TOKEN), lambda i: (0, 0)),
                  pl.BlockSpec(memory_space=pl.ANY)],
        out_specs=tok(d),
        out_shape=jax.ShapeDtypeStruct((tkn, d), F32),
        scratch_shapes=[pltpu.VMEM((2, APPLY_ROWS, 2 * d), F32), pltpu.SemaphoreType.DMA((2,))],
        compiler_params=_params("arbitrary"),
        name="peer_apply",
    )(ids, ids, gate, h2, x1, gate2, jnp.eye(ROWS_PER_TOKEN, dtype=F32), table)


def _rms_norm(x, gain):
    return x * lax.rsqrt(jnp.mean(x * x, axis=-1, keepdims=True) + EPS) * gain


def _l2_norm(x):
    return x * lax.rsqrt(jnp.sum(x * x, axis=-1, keepdims=True) + EPS)


def _gated_delta_rule(q, k, v, g, beta):
    b, h, s, dk = q.shape
    dv = v.shape[-1]
    c = DN_CHUNK
    n = s // c
    q = q.reshape(b, h, n, c, dk)
    k = k.reshape(b, h, n, c, dk)
    v = v.reshape(b, h, n, c, dv)
    g = jnp.cumsum(g.reshape(b, h, n, c), axis=-1)
    beta = beta.reshape(b, h, n, c)[..., None]
    idx = jnp.arange(c)
    lower_incl = idx[:, None] >= idx[None, :]
    lower_strict = idx[:, None] > idx[None, :]
    decay = jnp.exp(jnp.where(lower_incl, g[..., :, None] - g[..., None, :], -jnp.inf))
    kbeta = k * beta
    kk = jnp.einsum('bhnid,bhnjd->bhnij', kbeta, k)
    tri = jnp.eye(c, dtype=F32) + jnp.where(lower_strict, kk * decay, 0.0)
    rhs = jnp.concatenate([kbeta * jnp.exp(g)[..., None], v * beta], axis=-1)
    sol = lax.linalg.triangular_solve(tri, rhs, left_side=True, lower=True, unit_diagonal=True)
    w_c, u_c = sol[..., :dk], sol[..., dk:]
    attn = jnp.einsum('bhnid,bhnjd->bhnij', q, k) * decay
    q_dec = q * jnp.exp(g)[..., None]
    k_dec = k * jnp.exp(g[..., -1:] - g)[..., None]
    g_last = jnp.exp(g[..., -1])

    def step(state, xs):
        q_i, k_i, w_i, u_i, a_i, gl_i = xs
        v_new = u_i - jnp.einsum('bhcd,bhde->bhce', w_i, state)
        o_i = jnp.einsum('bhcd,bhde->bhce', q_i, state) + jnp.einsum('bhij,bhje->bhie', a_i, v_new)
        state = state * gl_i[..., None, None] + jnp.einsum('bhcd,bhce->bhde', k_i, v_new)
        return state, o_i

    xs = tuple(jnp.moveaxis(t, 2, 0) for t in (q_dec, k_dec, w_c, u_c, attn, g_last))
    _, o = lax.scan(step, jnp.zeros((b, h, dk, dv), F32), xs)
    return jnp.moveaxis(o, 0, 2).reshape(b, h, s, dv)


def _deltanet_jax(qkv, dz, bg, conv_w, a_log, dt_bias, out_gain):
    b, s, _ = qkv.shape
    conv = lax.conv_general_dilated(
        qkv, conv_w[:, None, :], window_strides=(1,), padding=[(DN_CONV - 1, 0)],
        dimension_numbers=("NWC", "WIO", "NWC"), feature_group_count=qkv.shape[-1])
    dq, dk, dv = jnp.split(jax.nn.silu(conv), 3, axis=-1)
    q = _l2_norm(dq.reshape(b, s, DN_HEADS, DN_HEAD_DIM)) * DN_HEAD_DIM ** -0.5
    k = _l2_norm(dk.reshape(b, s, DN_HEADS, DN_HEAD_DIM))
    v = dv.reshape(b, s, DN_HEADS, DN_HEAD_DIM)
    beta = jax.nn.sigmoid(bg[..., 0:DN_HEADS])
    g = -jnp.exp(a_log) * jax.nn.softplus(bg[..., DN_HEADS:2 * DN_HEADS] + dt_bias)
    o = _gated_delta_rule(q.transpose(0, 2, 1, 3), k.transpose(0, 2, 1, 3), v.transpose(0, 2, 1, 3),
                          g.transpose(0, 2, 1), beta.transpose(0, 2, 1)).transpose(0, 2, 1, 3)
    o = _rms_norm(o, out_gain) * jax.nn.silu(dz.reshape(b, s, DN_HEADS, DN_HEAD_DIM))
    return o.reshape(b, s, DN_WIDTH)


def _peer_jax(h, w_query, sub_keys, w_u, w_v):
    b, s, d = h.shape
    hb = h.reshape(b * s // PEER_TOKEN_BLOCK, PEER_TOKEN_BLOCK, d)

    def block(ht):
        t = ht.shape[0]
        q = (ht @ w_query).reshape(t, PEER_HEADS, 2, PEER_HALF)
        scores = jnp.einsum('thpk,hpnk->thpn', q, sub_keys)
        sv, si = lax.top_k(scores, PEER_TOPK)
        cand_s = sv[:, :, 0, :, None] + sv[:, :, 1, None, :]
        cand_i = si[:, :, 0, :, None] * PEER_N_KEYS + si[:, :, 1, None, :]
        top_s, top_j = lax.top_k(cand_s.reshape(t, PEER_HEADS, PEER_TOPK * PEER_TOPK), PEER_TOPK)
        expert = jnp.take_along_axis(cand_i.reshape(t, PEER_HEADS, PEER_TOPK * PEER_TOPK), top_j, axis=-1)
        gate = jax.nn.softmax(top_s, axis=-1)
        pre = jnp.einsum('td,thkd->thk', ht, w_u[expert])
        act = jax.nn.gelu(pre, approximate=False) * gate
        return jnp.einsum('thk,thkd->td', act, w_v[expert])

    return lax.map(block, hb).reshape(b, s, d)


def _regroup_w_in(w_in):
    bg0 = 4 * DN_WIDTH
    sb0 = bg0 + 2 * DN_HEADS
    pad = jnp.zeros((w_in.shape[0], LANES - 2 * DN_HEADS), w_in.dtype)
    return jnp.concatenate([w_in[:, :bg0], w_in[:, sb0:], w_in[:, bg0:sb0], pad], axis=1).astype(BF16)


def kernel(x, c, w_ada, b_ada, norm1_gain, w_in, dn_conv_w, dn_a_log, dn_dt_bias, dn_out_gain, sb_q_gain,
           sb_k_gain, sb_out_gain, w_out, norm2_gain, peer_w_query, peer_sub_keys, peer_w_u, peer_w_v):
    for l in range(w_ada.shape[0]):
        mod = _ada_mod(c, w_ada[l], b_ada[l])
        shift1, scale1, gate1, shift2, scale2, gate2 = jnp.split(mod[:, None, :], N_MOD, axis=-1)
        qkv, dz, sb, bg = _in_projection(x, norm1_gain[l], scale1, shift1, _regroup_w_in(w_in[l]))
        o_dn = _deltanet_jax(qkv, dz, bg, dn_conv_w[l], dn_a_log[l], dn_dt_bias[l], dn_out_gain[l])
        qn, kn, vb = _sb_prep(sb, sb_q_gain[l], sb_k_gain[l])
        o_sb = _sb_attention(qn, kn, vb, sb_out_gain[l])
        x1, h2 = _out_projection(o_dn, o_sb, x, gate1, w_out[l].astype(BF16), norm2_gain[l], scale2, shift2)
        b, s, d = x.shape
        h2 = h2.reshape(b * s, d)
        expert, gate = _peer_route(h2, peer_w_query[l], peer_sub_keys[l])
        table = jnp.concatenate([peer_w_u[l], peer_w_v[l]], axis=1)
        x = _peer_apply(expert, gate, h2, x1.reshape(b * s, d), gate2, table, s).reshape(b, s, d)
    return x
```

```python
import functools

import numpy as np
import jax
import jax.numpy as jnp
from jax import lax
from jax.experimental import pallas as pl
from jax.experimental.pallas import tpu as pltpu

F32 = jnp.float32
BF16 = jnp.bfloat16
HIGHEST = lax.Precision.HIGHEST

D_MODEL = 1024
DN_HEAD_DIM = 128
DN_WIDTH = 512
DN_HEADS = 4
DN_CONV = 4
DN_CHUNK = 64
SB_HEAD_DIM = 64
SB_WIDTH = 512
SB_HEADS = 8
PEER_HEADS = 8
PEER_N_KEYS = 128
PEER_HALF = 128
PEER_TOPK = 16
PEER_TOKEN_BLOCK = 128
N_MOD = 6
EPS = 1e-6

LANES = 128
VMEM_LIMIT = 56 * 1024 * 1024

QKV_COLS = 3 * DN_WIDTH
Z_OFF = QKV_COLS
SB_OFF = Z_OFF + DN_WIDTH
BG_OFF = SB_OFF + 3 * SB_WIDTH
IN_COLS_PADDED = BG_OFF + LANES

TS_PROJ = 256
SB_TILE = 256


def _params(*sem):
    return pltpu.CompilerParams(dimension_semantics=sem, vmem_limit_bytes=VMEM_LIMIT)


def _ada_kernel(c_ref, w_ref, b_ref, o_ref):
    c = c_ref[...]
    sc = c * jax.nn.sigmoid(c)
    o_ref[...] = jnp.dot(sc, w_ref[...], precision=HIGHEST, preferred_element_type=F32) + b_ref[...]


def _ada_mod(c, w_ada, b_ada):
    b, d = c.shape
    n = w_ada.shape[1]
    tn = 512
    return pl.pallas_call(
        _ada_kernel,
        grid=(n // tn,),
        in_specs=[pl.BlockSpec((b, d), lambda j: (0, 0)),
                  pl.BlockSpec((d, tn), lambda j: (0, j)),
                  pl.BlockSpec((1, tn), lambda j: (0, j))],
        out_specs=pl.BlockSpec((b, tn), lambda j: (0, j)),
        out_shape=jax.ShapeDtypeStruct((b, n), F32),
        compiler_params=_params("parallel"),
        name="ada_mod",
    )(c, w_ada, b_ada.reshape(1, n))


def _modulated_norm(x, gain, scale, shift):
    y = x * lax.rsqrt(jnp.mean(x * x, axis=-1, keepdims=True) + EPS)
    return (y * gain) * (1.0 + scale) + shift


def _inproj_kernel(x_ref, g_ref, sc_ref, sh_ref, w_ref, qkv_ref, z_ref, sb_ref, bg_ref):
    h = _modulated_norm(x_ref[0], g_ref[...], sc_ref[0], sh_ref[0]).astype(BF16)
    qkv_ref[0] = jnp.dot(h, w_ref[:, 0:Z_OFF], preferred_element_type=F32)
    z_ref[0] = jnp.dot(h, w_ref[:, Z_OFF:SB_OFF], preferred_element_type=F32)
    sb_ref[0] = jnp.dot(h, w_ref[:, SB_OFF:BG_OFF], preferred_element_type=F32)
    bg_ref[0] = jnp.dot(h, w_ref[:, BG_OFF:IN_COLS_PADDED], preferred_element_type=F32)


def _in_projection(x, gain, scale, shift, w):
    b, s, d = x.shape
    ts = TS_PROJ
    tok = lambda n: pl.BlockSpec((1, ts, n), lambda i, j: (i, j, 0))
    per_batch = pl.BlockSpec((1, 1, d), lambda i, j: (i, 0, 0))
    return pl.pallas_call(
        _inproj_kernel,
        grid=(b, s // ts),
        in_specs=[tok(d), pl.BlockSpec((1, d), lambda i, j: (0, 0)), per_batch, per_batch,
                  pl.BlockSpec((d, IN_COLS_PADDED), lambda i, j: (0, 0))],
        out_specs=[tok(QKV_COLS), tok(DN_WIDTH), tok(3 * SB_WIDTH), tok(LANES)],
        out_shape=[jax.ShapeDtypeStruct((b, s, QKV_COLS), F32),
                   jax.ShapeDtypeStruct((b, s, DN_WIDTH), F32),
                   jax.ShapeDtypeStruct((b, s, 3 * SB_WIDTH), F32),
                   jax.ShapeDtypeStruct((b, s, LANES), F32)],
        compiler_params=_params("parallel", "parallel"),
        name="in_projection",
    )(x, gain.reshape(1, d), scale, shift, w)


def _head_mean_square(x, blockdiag):
    xx = x * x
    hi = xx.astype(BF16)
    lo = (xx - hi.astype(F32)).astype(BF16)
    ss = jnp.dot(hi, blockdiag, preferred_element_type=F32) + jnp.dot(lo, blockdiag, preferred_element_type=F32)
    return ss * (1.0 / SB_HEAD_DIM)


def _sb_prep_kernel(sb_ref, bd_ref, gq_ref, gk_ref, q_ref, k_ref, v_ref):
    bd = bd_ref[...]
    q = sb_ref[0, :, 0:SB_WIDTH]
    k = sb_ref[0, :, SB_WIDTH:2 * SB_WIDTH]
    qn = q * lax.rsqrt(_head_mean_square(q, bd) + EPS) * gq_ref[...]
    kn = k * lax.rsqrt(_head_mean_square(k, bd) + EPS) * gk_ref[...]
    q_ref[0] = (qn * SB_HEAD_DIM ** -0.5).astype(BF16)
    k_ref[0] = kn.astype(BF16)
    v_ref[0] = sb_ref[0, :, 2 * SB_WIDTH:3 * SB_WIDTH].astype(BF16)


def _sb_prep(sb, q_gain, k_gain):
    b, s, _ = sb.shape
    ts = TS_PROJ
    head_of = np.arange(SB_WIDTH) // SB_HEAD_DIM
    blockdiag = jnp.asarray(head_of[:, None] == head_of[None, :], BF16)
    tok = lambda n: pl.BlockSpec((1, ts, n), lambda i, j: (i, j, 0))
    const = lambda r, c: pl.BlockSpec((r, c), lambda i, j: (0, 0))
    out = jax.ShapeDtypeStruct((b, s, SB_WIDTH), BF16)
    return pl.pallas_call(
        _sb_prep_kernel,
        grid=(b, s // ts),
        in_specs=[tok(3 * SB_WIDTH), const(SB_WIDTH, SB_WIDTH), const(1, SB_WIDTH), const(1, SB_WIDTH)],
        out_specs=[tok(SB_WIDTH)] * 3,
        out_shape=[out] * 3,
        compiler_params=_params("parallel", "parallel"),
        name="sb_prep",
    )(sb, blockdiag, jnp.tile(q_gain, SB_HEADS).reshape(1, SB_WIDTH), jnp.tile(k_gain, SB_HEADS).reshape(1, SB_WIDTH))


HEADS_PER_STEP = LANES // SB_HEAD_DIM


def _sb_kernel(q_ref, k_ref, v_ref, u_ref, g_ref, o_ref, acc_ref, carry_ref):
    qi = pl.program_id(2)
    j = pl.program_id(3)
    t = SB_TILE

    @pl.when(j == 0)
    def _():
        acc_ref[...] = jnp.zeros_like(acc_ref)
        carry_ref[...] = jnp.zeros_like(carry_ref)

    def step(diagonal):
        for h in range(HEADS_PER_STEP):
            lanes = slice(h * SB_HEAD_DIM, (h + 1) * SB_HEAD_DIM)
            q = q_ref[0, :, lanes]
            k = k_ref[0, :, lanes]
            v = v_ref[0, :, lanes]
            z = lax.dot_general(q, k, (((1,), (1,)), ((), ())), preferred_element_type=F32)
            log_beta = jnp.minimum(z, 0.0) - jnp.log1p(jnp.exp(-jnp.abs(z)))
            log_1m = log_beta - z
            if diagonal:
                causal = lax.broadcasted_iota(jnp.int32, (t, t), 1) < lax.broadcasted_iota(jnp.int32, (t, t), 0)
                log_1m = jnp.where(causal, log_1m, 0.0)
            hi = log_1m.astype(BF16)
            lo = (log_1m - hi.astype(F32)).astype(BF16)
            u = u_ref[...]
            after = jnp.dot(hi, u, preferred_element_type=F32) + jnp.dot(lo, u, preferred_element_type=F32)
            carry = carry_ref[h]
            a = jnp.exp(log_beta + after + carry)
            if diagonal:
                a = jnp.where(causal, a, 0.0)
            acc_ref[h] += jnp.dot(a.astype(BF16), v, preferred_element_type=F32)
            carry_ref[h] = carry + jnp.sum(log_1m, axis=-1, keepdims=True)

    @pl.when(j == 0)
    def _():
        step(True)

    @pl.when(jnp.logical_and(j > 0, j <= qi))
    def _():
        step(False)

    @pl.when(j == qi)
    def _():
        for h in range(HEADS_PER_STEP):
            o = acc_ref[h]
            o = o * lax.rsqrt(jnp.mean(o * o, axis=-1, keepdims=True) + EPS) * g_ref[...]
            o_ref[0, :, h * SB_HEAD_DIM:(h + 1) * SB_HEAD_DIM] = o


def _sb_attention(qn, kn, vb, out_gain):
    b, s, _ = qn.shape
    t = SB_TILE
    n = s // t
    idx = np.arange(t)
    suffix = jnp.asarray(idx[:, None] > idx[None, :], BF16)
    qspec = pl.BlockSpec((1, t, LANES), lambda bi, hp, qi, j: (bi, qi, hp))
    kspec = pl.BlockSpec((1, t, LANES), lambda bi, hp, qi, j: (bi, jnp.maximum(qi - j, 0), hp))
    return pl.pallas_call(
        _sb_kernel,
        grid=(b, SB_WIDTH // LANES, n, n),
        in_specs=[qspec, kspec, kspec,
                  pl.BlockSpec((t, t), lambda bi, hp, qi, j: (0, 0)),
                  pl.BlockSpec((1, SB_HEAD_DIM), lambda bi, hp, qi, j: (0, 0))],
        out_specs=qspec,
        out_shape=jax.ShapeDtypeStruct((b, s, SB_WIDTH), F32),
        scratch_shapes=[pltpu.VMEM((HEADS_PER_STEP, t, SB_HEAD_DIM), F32),
                        pltpu.VMEM((HEADS_PER_STEP, t, 1), F32)],
        compiler_params=_params("parallel", "parallel", "arbitrary", "arbitrary"),
        name="sb_attention",
    )(qn, kn, vb, suffix, out_gain.reshape(1, SB_HEAD_DIM))


def _outproj_kernel(odn_ref, osb_ref, x_ref, g1_ref, w_ref, n2_ref, sc2_ref, sh2_ref, wq_ref, x1_ref, h2_ref, pq_ref):
    mixed = (jnp.dot(odn_ref[0].astype(BF16), w_ref[0:DN_WIDTH, :], preferred_element_type=F32)
             + jnp.dot(osb_ref[0].astype(BF16), w_ref[DN_WIDTH:DN_WIDTH + SB_WIDTH, :], preferred_element_type=F32))
    x1 = x_ref[0] + g1_ref[0] * mixed
    x1_ref[0] = x1
    h2 = _modulated_norm(x1, n2_ref[...], sc2_ref[0], sh2_ref[0])
    h2_ref[0] = h2
    pq_ref[0] = jnp.dot(h2.astype(BF16), wq_ref[...], preferred_element_type=F32).astype(BF16)


def _out_projection(o_dn, o_sb, x, gate1, w_out, gain2, scale2, shift2, w_query):
    b, s, d = x.shape
    nq = w_query.shape[1]
    ts = TS_PROJ
    tok = lambda n: pl.BlockSpec((1, ts, n), lambda i, j: (i, j, 0))
    per_batch = pl.BlockSpec((1, 1, d), lambda i, j: (i, 0, 0))
    const = lambda r, c: pl.BlockSpec((r, c), lambda i, j: (0, 0))
    out = jax.ShapeDtypeStruct((b, s, d), F32)
    return pl.pallas_call(
        _outproj_kernel,
        grid=(b, s // ts),
        in_specs=[tok(DN_WIDTH), tok(SB_WIDTH), tok(d), per_batch, const(d, d), const(1, d), per_batch, per_batch,
                  const(d, nq)],
        out_specs=[tok(d), tok(d), tok(nq)],
        out_shape=[out, out, jax.ShapeDtypeStruct((b, s, nq), BF16)],
        compiler_params=_params("parallel", "parallel"),
        name="out_projection",
    )(o_dn, o_sb, x, gate1, w_out, gain2.reshape(1, d), scale2, shift2, w_query)


ROUTE_TOKENS = LANES


def _top_k_sublanes(scores, k):
    n, tokens = scores.shape
    row = lax.broadcasted_iota(jnp.int32, scores.shape, 0).astype(F32)
    rank = lax.broadcasted_iota(jnp.int32, (k, tokens), 0)

    def body(i, state):
        s, vals, rows = state
        m = jnp.max(s, axis=0, keepdims=True)
        pos = jnp.min(jnp.where(s == m, row, float(n)), axis=0, keepdims=True)
        hit = rank == i
        return jnp.where(row == pos, -jnp.inf, s), jnp.where(hit, m, vals), jnp.where(hit, pos, rows)

    zeros = jnp.zeros((k, tokens), F32)
    _, vals, rows = lax.fori_loop(0, k, body, (scores, zeros, zeros))
    return vals, rows


def _select_row(table, index):
    out = jnp.zeros_like(table)
    for r in range(table.shape[0]):
        out = out + jnp.where(index == float(r), table[r:r + 1, :], 0.0)
    return out


def _route_kernel(pq_ref, keys_ref, e_ref, g_ref):
    k = PEER_TOPK
    vals, ids = [], []
    for p in range(2):
        q = pq_ref[:, p * PEER_HALF:(p + 1) * PEER_HALF]
        s = lax.dot_general(keys_ref[0, p], q, (((1,), (1,)), ((), ())), preferred_element_type=F32)
        v, i = _top_k_sublanes(s, k)
        vals.append(v)
        ids.append(i)
    cand = jnp.concatenate([vals[0][a:a + 1, :] + vals[1] for a in range(k)], axis=0)
    top_s, top_row = _top_k_sublanes(cand, k)
    a = jnp.floor(top_row * (1.0 / k))
    b = top_row - a * k
    expert = _select_row(ids[0], a) * PEER_N_KEYS + _select_row(ids[1], b)
    ex = jnp.exp(top_s - jnp.max(top_s, axis=0, keepdims=True))
    g_ref[0] = ex / jnp.sum(ex, axis=0, keepdims=True)
    e_ref[0] = expert.astype(jnp.int32)


def _peer_route(pq, sub_keys):
    tkn = pq.shape[0]
    tt = ROUTE_TOKENS
    out = lambda dt: jax.ShapeDtypeStruct((PEER_HEADS, PEER_TOPK, tkn), dt)
    ospec = pl.BlockSpec((1, PEER_TOPK, tt), lambda i, hd: (hd, 0, i))
    expert, gate = pl.pallas_call(
        _route_kernel,
        grid=(tkn // tt, PEER_HEADS),
        in_specs=[pl.BlockSpec((tt, 2 * PEER_HALF), lambda i, hd: (i, hd)),
                  pl.BlockSpec((1, 2, PEER_N_KEYS, PEER_HALF), lambda i, hd: (hd, 0, 0, 0))],
        out_specs=[ospec, ospec],
        out_shape=[out(jnp.int32), out(F32)],
        compiler_params=_params("parallel", "arbitrary"),
        name="peer_route",
    )(pq, sub_keys.astype(BF16))
    flat = lambda a: a.transpose(2, 0, 1).reshape(tkn, PEER_HEADS * PEER_TOPK)
    return flat(expert), flat(gate)


APPLY_TOKENS = 8
ROWS_PER_TOKEN = PEER_HEADS * PEER_TOPK
APPLY_ROWS = APPLY_TOKENS * ROWS_PER_TOKEN


def _peer_apply_kernel(idx_ref, idx_next_ref, gate_ref, h_ref, x1_ref, g2_ref, eye_ref, tab_ref, o_ref,
                       buf_ref, sem_ref):
    i = pl.program_id(0)
    slot = i % 2
    d = h_ref.shape[-1]

    def issue(ids_ref, to_slot):
        def body(r, c):
            e = ids_ref[0, 0, r]
            pltpu.make_async_copy(tab_ref.at[pl.ds(e, 1)], buf_ref.at[to_slot, pl.ds(r, 1)],
                                  sem_ref.at[to_slot]).start()
            return c
        lax.fori_loop(0, APPLY_ROWS, body, 0, unroll=8)

    @pl.when(i == 0)
    def _():
        issue(idx_ref, 0)

    @pl.when(i + 1 < pl.num_programs(0))
    def _():
        issue(idx_next_ref, 1 - slot)

    pltpu.make_async_copy(tab_ref.at[pl.ds(0, APPLY_ROWS)], buf_ref.at[slot], sem_ref.at[slot]).wait()

    gate_t = lax.dot_general(eye_ref[...], gate_ref[...], (((1,), (1,)), ((), ())),
                             precision=HIGHEST, preferred_element_type=F32)
    tok_lane = lax.broadcasted_iota(jnp.int32, (ROWS_PER_TOKEN, APPLY_TOKENS), 1)
    pre = jnp.zeros((ROWS_PER_TOKEN, APPLY_TOKENS), F32)
    for t in range(APPLY_TOKENS):
        u = buf_ref[slot, pl.ds(t * ROWS_PER_TOKEN, ROWS_PER_TOKEN), 0:d]
        pre_t = jnp.sum(u * h_ref[t:t + 1, :], axis=-1, keepdims=True)
        pre = jnp.where(tok_lane == t, pre_t, pre)
    act = 0.5 * pre * (1.0 + lax.erf(pre * (2.0 ** -0.5))) * gate_t
    for t in range(APPLY_TOKENS):
        v = buf_ref[slot, pl.ds(t * ROWS_PER_TOKEN, ROWS_PER_TOKEN), d:2 * d]
        y =jnp.sum(act[:, t:t + 1] * v, axis=0, keepdims=True)
        o_ref[t:t + 1, :] = x1_ref[t:t + 1, :] + g2_ref[0] * y


def _peer_apply(expert, gate, h2, x1, gate2, table, tokens_per_batch):
    tkn, d = h2.shape
    tt = APPLY_TOKENS
    n = tkn // tt
    steps_per_batch = tokens_per_batch // tt
    ids = expert.reshape(n, 1, APPLY_ROWS)
    tok = lambda w: pl.BlockSpec((tt, w), lambda i: (i, 0))
    smem = lambda imap: pl.BlockSpec((1, 1, APPLY_ROWS), imap, memory_space=pltpu.SMEM)
    return pl.pallas_call(
        _peer_apply_kernel,
        grid=(n,),
        in_specs=[smem(lambda i: (i, 0, 0)),
                  smem(lambda i: (jnp.minimum(i + 1, n - 1), 0, 0)),
                  tok(ROWS_PER_TOKEN), tok(d), tok(d),
                  pl.BlockSpec((1, 1, d), lambda i: (i // steps_per_batch, 0, 0)),
                  pl.BlockSpec((ROWS_PER_TOKEN, ROWS_PER_TOKEN), lambda i: (0, 0)),
                  pl.BlockSpec(memory_space=pl.ANY)],
        out_specs=tok(d),
        out_shape=jax.ShapeDtypeStruct((tkn, d), F32),
        scratch_shapes=[pltpu.VMEM((2, APPLY_ROWS, 2 * d), F32), pltpu.SemaphoreType.DMA((2,))],
        compiler_params=_params("arbitrary"),
        name="peer_apply",
    )(ids, ids, gate, h2, x1, gate2, jnp.eye(ROWS_PER_TOKEN, dtype=F32), table)


DN_TOKENS = 512
CONV_HALO = 8
BETA_OFF, G_OFF = 0, DN_HEADS


def _dn_prep_kernel(x_ref, prev_ref, bg_ref, cw_ref, alog_ref, dtb_ref, q_ref, k_ref, v_ref, gb_ref):
    j = pl.program_id(1)
    x = x_ref[0]
    ts = x.shape[0]
    prev = jnp.where(j > 0, prev_ref[0], 0.0)
    ext = jnp.concatenate([prev, x], axis=0)
    acc = x * cw_ref[DN_CONV - 1:DN_CONV, :]
    for back in range(1, DN_CONV):
        shifted = pltpu.roll(ext, back, axis=0)[CONV_HALO:CONV_HALO + ts, :]
        acc = acc + shifted * cw_ref[DN_CONV - 1 - back:DN_CONV - back, :]
    y = acc * jax.nn.sigmoid(acc)
    for h in range(DN_HEADS):
        lanes = slice(h * DN_HEAD_DIM, (h + 1) * DN_HEAD_DIM)
        q = y[:, lanes]
        k = y[:, DN_WIDTH + h * DN_HEAD_DIM:DN_WIDTH + (h + 1) * DN_HEAD_DIM]
        q_ref[0, :, lanes] = q * lax.rsqrt(jnp.sum(q * q, axis=-1, keepdims=True) + EPS) * DN_HEAD_DIM ** -0.5
        k_ref[0, :, lanes] = k * lax.rsqrt(jnp.sum(k * k, axis=-1, keepdims=True) + EPS)
    v_ref[0] = y[:, 2 * DN_WIDTH:3 * DN_WIDTH]
    bg = bg_ref[0]
    sp_in = bg + dtb_ref[...]
    g = -jnp.exp(alog_ref[...]) * (jnp.maximum(sp_in, 0.0) + jnp.log1p(jnp.exp(-jnp.abs(sp_in))))
    lane = lax.broadcasted_iota(jnp.int32, bg.shape, 1)
    gb_ref[0] = jnp.where(lane < DN_HEADS, jax.nn.sigmoid(bg), g)


def _dn_prep(qkv, bg, conv_w, a_log, dt_bias):
    b, s, w = qkv.shape
    ts = DN_TOKENS
    tok = lambda n: pl.BlockSpec((1, ts, n), lambda i, j: (i, j, 0))
    halo = pl.BlockSpec((1, CONV_HALO, w), lambda i, j: (i, jnp.maximum(j * (ts // CONV_HALO) - 1, 0), 0))
    const = lambda r, c: pl.BlockSpec((r, c), lambda i, j: (0, 0))
    on_decay_lanes = lambda p: jnp.zeros((1, LANES), F32).at[0, DN_HEADS:2 * DN_HEADS].set(p)
    wide = jax.ShapeDtypeStruct((b, s, DN_WIDTH), F32)
    return pl.pallas_call(
        _dn_prep_kernel,
        grid=(b, s // ts),
        in_specs=[tok(w), halo, tok(LANES), const(DN_CONV, w), const(1, LANES), const(1, LANES)],
        out_specs=[tok(DN_WIDTH)] * 3 + [tok(LANES)],
        out_shape=[wide, wide, wide, jax.ShapeDtypeStruct((b, s, LANES), F32)],
        compiler_params=_params("parallel", "parallel"),
        name="dn_prep",
    )(qkv, qkv, bg, conv_w, on_decay_lanes(a_log), on_decay_lanes(dt_bias))


def _dot_hi(a, b):
    return jnp.dot(a, b, precision=HIGHEST, preferred_element_type=F32)


def _dn_core_kernel(q_ref, k_ref, v_ref, gb_ref, z_ref, gain_ref, o_ref, state_ref):
    c = DN_CHUNK

    @pl.when(pl.program_id(1) == 0)
    def _():
        state_ref[...] = jnp.zeros_like(state_ref)

    row = lax.broadcasted_iota(jnp.int32, (c, c), 0)
    col = lax.broadcasted_iota(jnp.int32, (c, c), 1)
    lower_incl = row >= col
    lower_strict = row > col
    eye = jnp.where(row == col, 1.0, 0.0)
    cumsum_mat = jnp.where(lower_incl, 1.0, 0.0)
    first_lane = jnp.where(lax.broadcasted_iota(jnp.int32, (c, LANES), 1) == 0, 1.0, 0.0)

    def chunk(ci, carry):
        rows = pl.ds(pl.multiple_of(ci * c, c), c)
        for h in range(DN_HEADS):
            lanes = slice(h * DN_HEAD_DIM, (h + 1) * DN_HEAD_DIM)
            q = q_ref[0, rows, lanes]
            k = k_ref[0, rows, lanes]
            v = v_ref[0, rows, lanes]
            g_col = jnp.broadcast_to(gb_ref[0, rows, G_OFF + h:G_OFF + h + 1], (c, LANES))
            beta = jnp.broadcast_to(gb_ref[0, rows, BETA_OFF + h:BETA_OFF + h + 1], (c, LANES))
            gc = _dot_hi(cumsum_mat, g_col)
            gc_row = lax.dot_general(first_lane, gc, (((1,), (1,)), ((), ())), precision=HIGHEST,
                                     preferred_element_type=F32)
            decay = jnp.where(lower_incl, jnp.exp(jnp.where(lower_incl, gc[:, :c] - gc_row, 0.0)), 0.0)
            kbeta = k * beta
            kk = lax.dot_general(kbeta, k, (((1,), (1,)), ((), ())), preferred_element_type=F32)
            n_pow = -jnp.where(lower_strict, kk * decay, 0.0)
            inv = eye + n_pow
            for _ in range(5):
                n_pow = _dot_hi(n_pow, n_pow)
                inv = inv + _dot_hi(inv, n_pow)
            e_gc = jnp.exp(gc)
            w_c = _dot_hi(inv, kbeta * e_gc)
            u_c = _dot_hi(inv, v * beta)
            attn = lax.dot_general(q, k, (((1,), (1,)), ((), ())), preferred_element_type=F32) * decay
            gc_last = gc[c - 1:c, :]
            state = state_ref[h]
            v_new = u_c - jnp.dot(w_c, state, preferred_element_type=F32)
            o = jnp.dot(q * e_gc, state, preferred_element_type=F32) + jnp.dot(attn, v_new, preferred_element_type=F32)
            k_dec = k * jnp.exp(gc_last - gc)
            state_ref[h] = state * jnp.exp(gc_last) + lax.dot_general(
                k_dec, v_new, (((0,), (0,)), ((), ())), preferred_element_type=F32)
            z = z_ref[0, rows, lanes]
            o = o * lax.rsqrt(jnp.mean(o * o, axis=-1, keepdims=True) + EPS) * gain_ref[...]
            o_ref[0, rows, lanes] = o * (z * jax.nn.sigmoid(z))
        return carry

    lax.fori_loop(0, DN_TOKENS // c, chunk, 0)


def _dn_core(q, k, v, gb, dz, out_gain):
    b, s, w = q.shape
    ts = DN_TOKENS
    tok = lambda n: pl.BlockSpec((1, ts, n), lambda i, j: (i, j, 0))
    return pl.pallas_call(
        _dn_core_kernel,
        grid=(b, s // ts),
        in_specs=[tok(w), tok(w), tok(w), tok(LANES), tok(w), pl.BlockSpec((1, DN_HEAD_DIM), lambda i, j: (0, 0))],
        out_specs=tok(w),
        out_shape=jax.ShapeDtypeStruct((b, s, w), F32),
        scratch_shapes=[pltpu.VMEM((DN_HEADS, DN_HEAD_DIM, DN_HEAD_DIM), F32)],
        compiler_params=_params("parallel", "arbitrary"),
        name="dn_core",
    )(q, k, v, gb, dz, out_gain.reshape(1, DN_HEAD_DIM))


def _deltanet(qkv, dz, bg, conv_w, a_log, dt_bias, out_gain):
    q, k, v, gb = _dn_prep(qkv, bg, conv_w, a_log, dt_bias)
    return _dn_core(q, k, v, gb, dz, out_gain)


def _regroup_w_in(w_in):
    bg0 = 4 * DN_WIDTH
    sb0 = bg0 + 2 * DN_HEADS
    pad = jnp.zeros((w_in.shape[0], LANES - 2 * DN_HEADS), w_in.dtype)
    return jnp.concatenate([w_in[:, :bg0], w_in[:, sb0:], w_in[:, bg0:sb0], pad], axis=1).astype(BF16)


def kernel(x, c, w_ada, b_ada, norm1_gain, w_in, dn_conv_w, dn_a_log, dn_dt_bias, dn_out_gain, sb_q_gain,
           sb_k_gain, sb_out_gain, w_out, norm2_gain, peer_w_query, peer_sub_keys, peer_w_u, peer_w_v):
    b, s, d = x.shape
    for l in range(w_ada.shape[0]):
        mod = _ada_mod(c, w_ada[l], b_ada[l])
        shift1, scale1, gate1, shift2, scale2, gate2 = jnp.split(mod[:, None, :], N_MOD, axis=-1)
        qkv, dz, sb, bg = _in_projection(x, norm1_gain[l], scale1, shift1, _regroup_w_in(w_in[l]))
        o_dn = _deltanet(qkv, dz, bg, dn_conv_w[l], dn_a_log[l], dn_dt_bias[l], dn_out_gain[l])
        qn, kn, vb = _sb_prep(sb, sb_q_gain[l], sb_k_gain[l])
        o_sb = _sb_attention(qn, kn, vb, sb_out_gain[l])
        x1, h2, pq = _out_projection(o_dn, o_sb, x, gate1, w_out[l].astype(BF16), norm2_gain[l], scale2, shift2,
                                     peer_w_query[l].astype(BF16))
        expert, gate = _peer_route(pq.reshape(b * s, -1), peer_sub_keys[l])
        table = jnp.concatenate([peer_w_u[l], peer_w_v[l]], axis=1)
        x = _peer_apply(expert, gate, h2.reshape(b * s, d), x1.reshape(b * s, d), gate2, table, s).reshape(b, s, d)
    return x
```

```python
import functools

import numpy as np
import jax
import jax.numpy as jnp
from jax import lax
from jax.experimental import pallas as pl
from jax.experimental.pallas import tpu as pltpu

F32 = jnp.float32
BF16 = jnp.bfloat16
HIGHEST = lax.Precision.HIGHEST

D_MODEL = 1024
DN_HEAD_DIM = 128
DN_WIDTH = 512
DN_HEADS = 4
DN_CONV = 4
DN_CHUNK = 64
SB_HEAD_DIM = 64
SB_WIDTH = 512
SB_HEADS = 8
PEER_HEADS = 8
PEER_N_KEYS = 128
PEER_HALF = 128
PEER_TOPK = 16
PEER_TOKEN_BLOCK = 128
N_MOD = 6
EPS = 1e-6

LANES = 128
VMEM_LIMIT = 56 * 1024 * 1024

QKV_COLS = 3 * DN_WIDTH
Z_OFF = QKV_COLS
SB_OFF = Z_OFF + DN_WIDTH
BG_OFF = SB_OFF + 3 * SB_WIDTH
IN_COLS_PADDED = BG_OFF + LANES

TS_PROJ = 256
SB_TILE = 256


def _params(*sem):
    return pltpu.CompilerParams(dimension_semantics=sem, vmem_limit_bytes=VMEM_LIMIT)


def _ada_kernel(c_ref, w_ref, b_ref, o_ref):
    c = c_ref[...]
    sc = c * jax.nn.sigmoid(c)
    o_ref[...] = jnp.dot(sc, w_ref[...], precision=HIGHEST, preferred_element_type=F32) + b_ref[...]


def _ada_mod(c, w_ada, b_ada):
    b, d = c.shape
    n = w_ada.shape[1]
    tn = 512
    return pl.pallas_call(
        _ada_kernel,
        grid=(n // tn,),
        in_specs=[pl.BlockSpec((b, d), lambda j: (0, 0)),
                  pl.BlockSpec((d, tn), lambda j: (0, j)),
                  pl.BlockSpec((1, tn), lambda j: (0, j))],
        out_specs=pl.BlockSpec((b, tn), lambda j: (0, j)),
        out_shape=jax.ShapeDtypeStruct((b, n), F32),
        compiler_params=_params("parallel"),
        name="ada_mod",
    )(c, w_ada, b_ada.reshape(1, n))


def _modulated_norm(x, gain, scale, shift):
    y = x * lax.rsqrt(jnp.mean(x * x, axis=-1, keepdims=True) + EPS)
    return (y * gain) * (1.0 + scale) + shift


def _inproj_kernel(x_ref, g_ref, sc_ref, sh_ref, w_ref, qkv_ref, z_ref, sb_ref, bg_ref):
    h = _modulated_norm(x_ref[0], g_ref[...], sc_ref[0], sh_ref[0]).astype(BF16)
    qkv_ref[0] = jnp.dot(h, w_ref[:, 0:Z_OFF], preferred_element_type=F32)
    z_ref[0] = jnp.dot(h, w_ref[:, Z_OFF:SB_OFF], preferred_element_type=F32)
    sb_ref[0] = jnp.dot(h, w_ref[:, SB_OFF:BG_OFF], preferred_element_type=F32)
    bg_ref[0] = jnp.dot(h, w_ref[:, BG_OFF:IN_COLS_PADDED], preferred_element_type=F32)


def _in_projection(x, gain, scale, shift, w):
    b, s, d = x.shape
    ts = TS_PROJ
    tok = lambda n: pl.BlockSpec((1, ts, n), lambda i, j: (i, j, 0))
    per_batch = pl.BlockSpec((1, 1, d), lambda i, j: (i, 0, 0))
    return pl.pallas_call(
        _inproj_kernel,
        grid=(b, s // ts),
        in_specs=[tok(d), pl.BlockSpec((1, d), lambda i, j: (0, 0)), per_batch, per_batch,
                  pl.BlockSpec((d, IN_COLS_PADDED), lambda i, j: (0, 0))],
        out_specs=[tok(QKV_COLS), tok(DN_WIDTH), tok(3 * SB_WIDTH), tok(LANES)],
        out_shape=[jax.ShapeDtypeStruct((b, s, QKV_COLS), F32),
                   jax.ShapeDtypeStruct((b, s, DN_WIDTH), F32),
                   jax.ShapeDtypeStruct((b, s, 3 * SB_WIDTH), F32),
                   jax.ShapeDtypeStruct((b, s, LANES), F32)],
        compiler_params=_params("parallel", "parallel"),
        name="in_projection",
    )(x, gain.reshape(1, d), scale, shift, w)


def _head_mean_square(x, blockdiag):
    xx = x * x
    hi = xx.astype(BF16)
    lo = (xx - hi.astype(F32)).astype(BF16)
    ss = jnp.dot(hi, blockdiag, preferred_element_type=F32) + jnp.dot(lo, blockdiag, preferred_element_type=F32)
    return ss * (1.0 / SB_HEAD_DIM)


def _sb_prep_kernel(sb_ref, bd_ref, gq_ref, gk_ref, q_ref, k_ref, v_ref):
    bd = bd_ref[...]
    q = sb_ref[0, :, 0:SB_WIDTH]
    k = sb_ref[0, :, SB_WIDTH:2 * SB_WIDTH]
    qn = q * lax.rsqrt(_head_mean_square(q, bd) + EPS) * gq_ref[...]
    kn = k * lax.rsqrt(_head_mean_square(k, bd) + EPS) * gk_ref[...]
    q_ref[0] = (qn * SB_HEAD_DIM ** -0.5).astype(BF16)
    k_ref[0] = kn.astype(BF16)
    v_ref[0] = sb_ref[0, :, 2 * SB_WIDTH:3 * SB_WIDTH].astype(BF16)


def _sb_prep(sb, q_gain, k_gain):
    b, s, _ = sb.shape
    ts = TS_PROJ
    head_of = np.arange(SB_WIDTH) // SB_HEAD_DIM
    blockdiag = jnp.asarray(head_of[:, None] == head_of[None, :], BF16)
    tok = lambda n: pl.BlockSpec((1, ts, n), lambda i, j: (i, j, 0))
    const = lambda r, c: pl.BlockSpec((r, c), lambda i, j: (0, 0))
    out = jax.ShapeDtypeStruct((b, s, SB_WIDTH), BF16)
    return pl.pallas_call(
        _sb_prep_kernel,
        grid=(b, s // ts),
        in_specs=[tok(3 * SB_WIDTH), const(SB_WIDTH, SB_WIDTH), const(1, SB_WIDTH), const(1, SB_WIDTH)],
        out_specs=[tok(SB_WIDTH)] * 3,
        out_shape=[out] * 3,
        compiler_params=_params("parallel", "parallel"),
        name="sb_prep",
    )(sb, blockdiag, jnp.tile(q_gain, SB_HEADS).reshape(1, SB_WIDTH), jnp.tile(k_gain, SB_HEADS).reshape(1, SB_WIDTH))


HEADS_PER_STEP = LANES // SB_HEAD_DIM


def _sb_kernel(qi_ref, kj_ref, q_ref, k_ref, v_ref, u_ref, g_ref, o_ref, acc_ref, carry_ref):
    p = pl.program_id(2)
    qi = qi_ref[p]
    kj = kj_ref[p]
    t = SB_TILE

    def step(diagonal):
        for h in range(HEADS_PER_STEP):
            lanes = slice(h * SB_HEAD_DIM, (h + 1) * SB_HEAD_DIM)
            q = q_ref[0, :, lanes]
            k = k_ref[0, :, lanes]
            v = v_ref[0, :, lanes]
            z = lax.dot_general(q, k, (((1,), (1,)), ((), ())), preferred_element_type=F32)
            log_beta = jnp.minimum(z, 0.0) - jnp.log1p(jnp.exp(-jnp.abs(z)))
            log_1m = log_beta - z
            if diagonal:
                causal = lax.broadcasted_iota(jnp.int32, (t, t), 1) < lax.broadcasted_iota(jnp.int32, (t, t), 0)
                log_1m = jnp.where(causal, log_1m, 0.0)
            hi = log_1m.astype(BF16)
            lo = (log_1m - hi.astype(F32)).astype(BF16)
            u = u_ref[...]
            after = jnp.dot(hi, u, preferred_element_type=F32) + jnp.dot(lo, u, preferred_element_type=F32)
            if diagonal:
                a = jnp.where(causal, jnp.exp(log_beta + after), 0.0)
                acc_ref[h] = jnp.dot(a.astype(BF16), v, preferred_element_type=F32)
                carry_ref[h] = jnp.sum(log_1m, axis=-1, keepdims=True)
            else:
                carry = carry_ref[h]
                a = jnp.exp(log_beta + after + carry)
                acc_ref[h] += jnp.dot(a.astype(BF16), v, preferred_element_type=F32)
                carry_ref[h] = carry + jnp.sum(log_1m, axis=-1, keepdims=True)

    @pl.when(kj == qi)
    def _():
        step(True)

    @pl.when(kj < qi)
    def _():
        step(False)

    @pl.when(kj == 0)
    def _():
        for h in range(HEADS_PER_STEP):
            o = acc_ref[h]
            o = o * lax.rsqrt(jnp.mean(o * o, axis=-1, keepdims=True) + EPS) * g_ref[...]
            o_ref[0, :, h * SB_HEAD_DIM:(h + 1) * SB_HEAD_DIM] = o


def _sb_attention(qn, kn, vb, out_gain):
    b, s, _ = qn.shape
    t = SB_TILE
    n = s // t
    idx = np.arange(t)
    suffix = jnp.asarray(idx[:, None] > idx[None, :], BF16)
    pairs = [(qi, kj) for qi in range(n) for kj in range(qi, -1, -1)]
    qi_of = jnp.asarray([p[0] for p in pairs], jnp.int32)
    kj_of = jnp.asarray([p[1] for p in pairs], jnp.int32)
    qspec = pl.BlockSpec((1, t, LANES), lambda bi, hp, p, qi_r, kj_r: (bi, qi_r[p], hp))
    kspec = pl.BlockSpec((1, t, LANES), lambda bi, hp, p, qi_r, kj_r: (bi, kj_r[p], hp))
    return pl.pallas_call(
        _sb_kernel,
        grid_spec=pltpu.PrefetchScalarGridSpec(
            num_scalar_prefetch=2,
            grid=(b, SB_WIDTH // LANES, len(pairs)),
            in_specs=[qspec, kspec, kspec,
                      pl.BlockSpec((t, t), lambda bi, hp, p, qi_r, kj_r: (0, 0)),
                      pl.BlockSpec((1, SB_HEAD_DIM), lambda bi, hp, p, qi_r, kj_r: (0, 0))],
            out_specs=qspec,
            scratch_shapes=[pltpu.VMEM((HEADS_PER_STEP, t, SB_HEAD_DIM), F32),
                            pltpu.VMEM((HEADS_PER_STEP, t, 1), F32)]),
        out_shape=jax.ShapeDtypeStruct((b, s, SB_WIDTH), F32),
        compiler_params=_params("parallel", "parallel", "arbitrary"),
        name="sb_attention",
    )(qi_of, kj_of, qn, kn, vb, suffix, out_gain.reshape(1, SB_HEAD_DIM))


def _outproj_kernel(odn_ref, osb_ref, x_ref, g1_ref, w_ref, n2_ref, sc2_ref, sh2_ref, wq_ref, x1_ref, h2_ref, pq_ref):
    mixed = (jnp.dot(odn_ref[0].astype(BF16), w_ref[0:DN_WIDTH, :], preferred_element_type=F32)
             + jnp.dot(osb_ref[0].astype(BF16), w_ref[DN_WIDTH:DN_WIDTH + SB_WIDTH, :], preferred_element_type=F32))
    x1 = x_ref[0] + g1_ref[0] * mixed
    x1_ref[0] = x1
    h2 = _modulated_norm(x1, n2_ref[...], sc2_ref[0], sh2_ref[0])
    h2_ref[0] = h2
    pq_ref[0] = jnp.dot(h2.astype(BF16), wq_ref[...], preferred_element_type=F32).astype(BF16)


def _out_projection(o_dn, o_sb, x, gate1, w_out, gain2, scale2, shift2, w_query):
    b, s, d = x.shape
    nq = w_query.shape[1]
    ts = TS_PROJ
    tok = lambda n: pl.BlockSpec((1, ts, n), lambda i, j: (i, j, 0))
    per_batch = pl.BlockSpec((1, 1, d), lambda i, j: (i, 0, 0))
    const = lambda r, c: pl.BlockSpec((r, c), lambda i, j: (0, 0))
    out = jax.ShapeDtypeStruct((b, s, d), F32)
    return pl.pallas_call(
        _outproj_kernel,
        grid=(b, s // ts),
        in_specs=[tok(DN_WIDTH), tok(SB_WIDTH), tok(d), per_batch, const(d, d), const(1, d), per_batch, per_batch,
                  const(d, nq)],
        out_specs=[tok(d), tok(d), tok(nq)],
        out_shape=[out, out, jax.ShapeDtypeStruct((b, s, nq), BF16)],
        compiler_params=_params("parallel", "parallel"),
        name="out_projection",
    )(o_dn, o_sb, x, gate1, w_out, gain2.reshape(1, d), scale2, shift2, w_query)


ROUTE_TOKENS = LANES


def _top_k_sublanes(scores, k):
    n, tokens = scores.shape
    row = lax.broadcasted_iota(jnp.int32, scores.shape, 0).astype(F32)
    rank = lax.broadcasted_iota(jnp.int32, (k, tokens), 0)

    def body(i, state):
        s, vals, rows = state
        m = jnp.max(s, axis=0, keepdims=True)
        pos = jnp.min(jnp.where(s == m, row, float(n)), axis=0, keepdims=True)
        hit = rank == i
        return jnp.where(row == pos, -jnp.inf, s), jnp.where(hit, m, vals), jnp.where(hit, pos, rows)

    zeros = jnp.zeros((k, tokens), F32)
    _, vals, rows = lax.fori_loop(0, k, body, (scores, zeros, zeros))
    return vals, rows


def _select_row(table, index):
    out = jnp.zeros_like(table)
    for r in range(table.shape[0]):
        out = out + jnp.where(index == float(r), table[r:r + 1, :], 0.0)
    return out


def _route_kernel(pq_ref, keys_ref, e_ref, g_ref):
    k = PEER_TOPK
    vals, ids = [], []
    for p in range(2):
        q = pq_ref[:, p * PEER_HALF:(p + 1) * PEER_HALF]
        s = lax.dot_general(keys_ref[0, p], q, (((1,), (1,)), ((), ())), preferred_element_type=F32)
        v, i = _top_k_sublanes(s, k)
        vals.append(v)
        ids.append(i)
    cand = jnp.concatenate([vals[0][a:a + 1, :] + vals[1] for a in range(k)], axis=0)
    top_s, top_row = _top_k_sublanes(cand, k)
    a = jnp.floor(top_row * (1.0 / k))
    b = top_row - a * k
    expert = _select_row(ids[0], a) * PEER_N_KEYS + _select_row(ids[1], b)
    ex = jnp.exp(top_s - jnp.max(top_s, axis=0, keepdims=True))
    g_ref[0] = ex / jnp.sum(ex, axis=0, keepdims=True)
    e_ref[0] = expert.astype(jnp.int32)


def _peer_route(pq, sub_keys):
    tkn = pq.shape[0]
    tt = ROUTE_TOKENS
    out = lambda dt: jax.ShapeDtypeStruct((PEER_HEADS, PEER_TOPK, tkn), dt)
    ospec = pl.BlockSpec((1, PEER_TOPK, tt), lambda i, hd: (hd, 0, i))
    expert, gate = pl.pallas_call(
        _route_kernel,
        grid=(tkn // tt, PEER_HEADS),
        in_specs=[pl.BlockSpec((tt, 2 * PEER_HALF), lambda i, hd: (i, hd)),
                  pl.BlockSpec((1, 2, PEER_N_KEYS, PEER_HALF), lambda i, hd: (hd, 0, 0, 0))],
        out_specs=[ospec, ospec],
        out_shape=[out(jnp.int32), out(F32)],
        compiler_params=_params("parallel", "arbitrary"),
        name="peer_route",
    )(pq, sub_keys.astype(BF16))
    flat = lambda a: a.transpose(2, 0, 1).reshape(tkn, PEER_HEADS * PEER_TOPK)
    return flat(expert), flat(gate)


APPLY_TOKENS = 8
ROWS_PER_TOKEN = PEER_HEADS * PEER_TOPK
APPLY_ROWS = APPLY_TOKENS * ROWS_PER_TOKEN
SUBLANES = 8
ROW_GROUPS = ROWS_PER_TOKEN // SUBLANES


def _column_sums_of_8(p):
    sub = lax.broadcasted_iota(jnp.int32, (SUBLANES, LANES), 0)
    lo4 = sub < 4
    b = [jnp.where(lo4, p[j], p[j + 4]) + pltpu.roll(jnp.where(lo4, p[j + 4], p[j]), 4, axis=0) for j in range(4)]
    lo2 = (sub & 2) == 0
    c = [jnp.where(lo2, b[j] + pltpu.roll(b[j], 6, axis=0), b[j + 2] + pltpu.roll(b[j + 2], 2, axis=0))
         for j in range(2)]
    even = (sub & 1) == 0
    return jnp.where(even, c[0] + pltpu.roll(c[0], 7, axis=0), c[1] + pltpu.roll(c[1], 1, axis=0))


def _peer_apply_kernel(idx_ref, idx_next_ref, gate_ref, h_ref, x1_ref, g2_ref, eye_ref, tab_ref, o_ref,
                       buf_ref, actb_ref, sem_ref):
    i = pl.program_id(0)
    slot = i % 2

    def issue(ids_ref, to_slot):
        def body(g, c):
            for j in range(SUBLANES):
                r = g * SUBLANES + j
                pltpu.make_async_copy(tab_ref.at[ids_ref[0, 0, r]], buf_ref.at[to_slot, r],
                                      sem_ref.at[to_slot]).start()
            return c
        lax.fori_loop(0, APPLY_ROWS // SUBLANES, body, 0)

    @pl.when(i == 0)
    def _():
        issue(idx_ref, 0)

    @pl.when(i + 1 < pl.num_programs(0))
    def _():
        issue(idx_next_ref, 1 - slot)

    pltpu.make_async_copy(tab_ref.at[pl.ds(0, APPLY_ROWS)], buf_ref.at[slot], sem_ref.at[slot]).wait()

    gate_t = lax.dot_general(eye_ref[...], gate_ref[...], (((1,), (1,)), ((), ())),
                             precision=HIGHEST, preferred_element_type=F32)
    tok_lane = lax.broadcasted_iota(jnp.int32, (ROWS_PER_TOKEN, APPLY_TOKENS), 1)
    pre = jnp.zeros((ROWS_PER_TOKEN, APPLY_TOKENS), F32)
    for t in range(APPLY_TOKENS):
        h = h_ref[t]
        cols = []
        for g in range(ROW_GROUPS):
            base = t * ROWS_PER_TOKEN + g * SUBLANES
            prods = [buf_ref[slot, base + r, 0:SUBLANES, :] * h for r in range(SUBLANES)]
            cols.append(jnp.sum(_column_sums_of_8(prods), axis=-1, keepdims=True))
        pre = jnp.where(tok_lane == t, jnp.concatenate(cols, axis=0), pre)
    act = 0.5 * pre * (1.0 + lax.erf(pre * (2.0 ** -0.5))) * gate_t
    for t in range(APPLY_TOKENS):
        actb_ref[...] = jnp.broadcast_to(act[:, t:t + 1], (ROWS_PER_TOKEN, LANES))
        parts = [jnp.zeros((SUBLANES, LANES), F32) for _ in range(4)]
        for k in range(ROWS_PER_TOKEN):
            v = buf_ref[slot, t * ROWS_PER_TOKEN + k, SUBLANES:2 * SUBLANES, :]
            parts[k % 4] = parts[k % 4] + actb_ref[k:k + 1, :] * v
        y = (parts[0] + parts[1]) + (parts[2] + parts[3])
        o_ref[t] = x1_ref[t] + g2_ref[0] * y


def _peer_apply(expert, gate, h2, x1, gate2, table, tokens_per_batch):
    tkn, d = h2.shape
    tt = APPLY_TOKENS
    n = tkn // tt
    steps_per_batch = tokens_per_batch // tt
    fold = d // LANES
    assert fold == SUBLANES
    ids = expert.reshape(n, 1, APPLY_ROWS)
    folded = lambda a: a.reshape(a.shape[0], fold, LANES)
    tok3 = pl.BlockSpec((tt, fold, LANES), lambda i: (i, 0, 0))
    smem = lambda imap: pl.BlockSpec((1, 1, APPLY_ROWS), imap, memory_space=pltpu.SMEM)
    out = pl.pallas_call(
        _peer_apply_kernel,
        grid=(n,),
        in_specs=[smem(lambda i: (i, 0, 0)),
                  smem(lambda i: (jnp.minimum(i + 1, n - 1), 0, 0)),
                  pl.BlockSpec((tt, ROWS_PER_TOKEN), lambda i: (i, 0)), tok3, tok3,
                  pl.BlockSpec((1, fold, LANES), lambda i: (i // steps_per_batch, 0, 0)),
                  pl.BlockSpec((ROWS_PER_TOKEN, ROWS_PER_TOKEN), lambda i: (0, 0)),
                  pl.BlockSpec(memory_space=pl.ANY)],
        out_specs=tok3,
        out_shape=jax.ShapeDtypeStruct((tkn, fold, LANES), F32),
        scratch_shapes=[pltpu.VMEM((2, APPLY_ROWS, 2 * fold, LANES), F32),
                        pltpu.VMEM((ROWS_PER_TOKEN, LANES), F32),
                        pltpu.SemaphoreType.DMA((2,))],
        compiler_params=_params("arbitrary"),
        name="peer_apply",
    )(ids, ids, gate, folded(h2), folded(x1), folded(gate2.reshape(-1, d)), jnp.eye(ROWS_PER_TOKEN, dtype=F32),
      table.reshape(table.shape[0], 2 * fold, LANES))
    return out.reshape(tkn, d)


DN_TOKENS = 512
CONV_HALO = 8
BETA_OFF, G_OFF = 0, DN_HEADS


def _dn_prep_kernel(x_ref, prev_ref, bg_ref, cw_ref, alog_ref, dtb_ref, q_ref, k_ref, v_ref, gb_ref):
    j = pl.program_id(1)
    x = x_ref[0]
    ts = x.shape[0]
    prev = jnp.where(j > 0, prev_ref[0], 0.0)
    ext = jnp.concatenate([prev, x], axis=0)
    acc = x * cw_ref[DN_CONV - 1:DN_CONV, :]
    for back in range(1, DN_CONV):
        shifted = pltpu.roll(ext, back, axis=0)[CONV_HALO:CONV_HALO + ts, :]
        acc = acc + shifted * cw_ref[DN_CONV - 1 - back:DN_CONV - back, :]
    y = acc * jax.nn.sigmoid(acc)
    for h in range(DN_HEADS):
        lanes = slice(h * DN_HEAD_DIM, (h + 1) * DN_HEAD_DIM)
        q = y[:, lanes]
        k = y[:, DN_WIDTH + h * DN_HEAD_DIM:DN_WIDTH + (h + 1) * DN_HEAD_DIM]
        q_ref[0, :, lanes] = q * lax.rsqrt(jnp.sum(q * q, axis=-1, keepdims=True) + EPS) * DN_HEAD_DIM ** -0.5
        k_ref[0, :, lanes] = k * lax.rsqrt(jnp.sum(k * k, axis=-1, keepdims=True) + EPS)
    v_ref[0] = y[:, 2 * DN_WIDTH:3 * DN_WIDTH]
    bg = bg_ref[0]
    sp_in = bg + dtb_ref[...]
    g = -jnp.exp(alog_ref[...]) * (jnp.maximum(sp_in, 0.0) + jnp.log1p(jnp.exp(-jnp.abs(sp_in))))
    lane = lax.broadcasted_iota(jnp.int32, bg.shape, 1)
    gb_ref[0] = jnp.where(lane < DN_HEADS, jax.nn.sigmoid(bg), g)


def _dn_prep(qkv, bg, conv_w, a_log, dt_bias):
    b, s, w = qkv.shape
    ts = DN_TOKENS
    tok = lambda n: pl.BlockSpec((1, ts, n), lambda i, j: (i, j, 0))
    halo = pl.BlockSpec((1, CONV_HALO, w), lambda i, j: (i, jnp.maximum(j * (ts // CONV_HALO) - 1, 0), 0))
    const = lambda r, c: pl.BlockSpec((r, c), lambda i, j: (0, 0))
    on_decay_lanes = lambda p: jnp.zeros((1, LANES), F32).at[0, DN_HEADS:2 * DN_HEADS].set(p)
    wide = jax.ShapeDtypeStruct((b, s, DN_WIDTH), F32)
    return pl.pallas_call(
        _dn_prep_kernel,
        grid=(b, s // ts),
        in_specs=[tok(w), halo, tok(LANES), const(DN_CONV, w), const(1, LANES), const(1, LANES)],
        out_specs=[tok(DN_WIDTH)] * 3 + [tok(LANES)],
        out_shape=[wide, wide, wide, jax.ShapeDtypeStruct((b, s, LANES), F32)],
        compiler_params=_params("parallel", "parallel"),
        name="dn_prep",
    )(qkv, qkv, bg, conv_w, on_decay_lanes(a_log), on_decay_lanes(dt_bias))


def _dot_hi(a, b):
    return jnp.dot(a, b, precision=HIGHEST, preferred_element_type=F32)


def _dn_core_kernel(q_ref, k_ref, v_ref, gb_ref, z_ref, gain_ref, o_ref, state_ref):
    c = DN_CHUNK

    @pl.when(pl.program_id(1) == 0)
    def _():
        state_ref[...] = jnp.zeros_like(state_ref)

    row = lax.broadcasted_iota(jnp.int32, (c, c), 0)
    col = lax.broadcasted_iota(jnp.int32, (c, c), 1)
    lower_incl = row >= col
    lower_strict = row > col
    eye = jnp.where(row == col, 1.0, 0.0)
    cumsum_mat = jnp.where(lower_incl, 1.0, 0.0)
    first_lane = jnp.where(lax.broadcasted_iota(jnp.int32, (c, LANES), 1) == 0, 1.0, 0.0)

    def chunk(ci, carry):
        rows = pl.ds(pl.multiple_of(ci * c, c), c)
        for h in range(DN_HEADS):
            lanes = slice(h * DN_HEAD_DIM, (h + 1) * DN_HEAD_DIM)
            q = q_ref[0, rows, lanes]
            k = k_ref[0, rows, lanes]
            v = v_ref[0, rows, lanes]
            g_col = jnp.broadcast_to(gb_ref[0, rows, G_OFF + h:G_OFF + h + 1], (c, LANES))
            beta = jnp.broadcast_to(gb_ref[0, rows, BETA_OFF + h:BETA_OFF + h + 1], (c, LANES))
            gc = _dot_hi(cumsum_mat, g_col)
            gc_row = lax.dot_general(first_lane, gc, (((1,), (1,)), ((), ())), precision=HIGHEST,
                                     preferred_element_type=F32)
            decay = jnp.where(lower_incl, jnp.exp(jnp.where(lower_incl, gc[:, :c] - gc_row, 0.0)), 0.0)
            kbeta = k * beta
            kk = lax.dot_general(kbeta, k, (((1,), (1,)), ((), ())), preferred_element_type=F32)
            n_pow = -jnp.where(lower_strict, kk * decay, 0.0)
            inv = eye + n_pow
            for _ in range(5):
                n_pow = _dot_hi(n_pow, n_pow)
                inv = inv + _dot_hi(inv, n_pow)
            e_gc = jnp.exp(gc)
            w_c = _dot_hi(inv, kbeta * e_gc)
            u_c = _dot_hi(inv, v * beta)
            attn = lax.dot_general(q, k, (((1,), (1,)), ((), ())), preferred_element_type=F32) * decay
            gc_last = gc[c - 1:c, :]
            state = state_ref[h]
            v_new = u_c - jnp.dot(w_c, state, preferred_element_type=F32)
            o = jnp.dot(q * e_gc, state, preferred_element_type=F32) + jnp.dot(attn, v_new, preferred_element_type=F32)
            k_dec = k * jnp.exp(gc_last - gc)
            state_ref[h] = state * jnp.exp(gc_last) + lax.dot_general(
                k_dec, v_new, (((0,), (0,)), ((), ())), preferred_element_type=F32)
            z = z_ref[0, rows, lanes]
            o = o * lax.rsqrt(jnp.mean(o * o, axis=-1, keepdims=True) + EPS) * gain_ref[...]
            o_ref[0, rows, lanes] = o * (z * jax.nn.sigmoid(z))
        return carry

    lax.fori_loop(0, DN_TOKENS // c, chunk, 0)


def _dn_core(q, k, v, gb, dz, out_gain):
    b, s, w = q.shape
    ts = DN_TOKENS
    tok = lambda n: pl.BlockSpec((1, ts, n), lambda i, j: (i, j, 0))
    return pl.pallas_call(
        _dn_core_kernel,
        grid=(b, s // ts),
        in_specs=[tok(w), tok(w), tok(w), tok(LANES), tok(w), pl.BlockSpec((1, DN_HEAD_DIM), lambda i, j: (0, 0))],
        out_specs=tok(w),
        out_shape=jax.ShapeDtypeStruct((b, s, w), F32),
        scratch_shapes=[pltpu.VMEM((DN_HEADS, DN_HEAD_DIM, DN_HEAD_DIM), F32)],
        compiler_params=_params("parallel", "arbitrary"),
        name="dn_core",
    )(q, k, v, gb, dz, out_gain.reshape(1, DN_HEAD_DIM))


def _deltanet(qkv, dz, bg, conv_w, a_log, dt_bias, out_gain):
    q, k, v, gb = _dn_prep(qkv, bg, conv_w, a_log, dt_bias)
    return _dn_core(q, k, v, gb, dz, out_gain)


def _regroup_w_in(w_in):
    bg0 = 4 * DN_WIDTH
    sb0 = bg0 + 2 * DN_HEADS
    pad = jnp.zeros((w_in.shape[0], LANES - 2 * DN_HEADS), w_in.dtype)
    return jnp.concatenate([w_in[:, :bg0], w_in[:, sb0:], w_in[:, bg0:sb0], pad], axis=1).astype(BF16)


def kernel(x, c, w_ada, b_ada, norm1_gain, w_in, dn_conv_w, dn_a_log, dn_dt_bias, dn_out_gain, sb_q_gain,
           sb_k_gain, sb_out_gain, w_out, norm2_gain, peer_w_query, peer_sub_keys, peer_w_u, peer_w_v):
    b, s, d = x.shape
    for l in range(w_ada.shape[0]):
        mod = _ada_mod(c, w_ada[l], b_ada[l])
        shift1, scale1, gate1, shift2, scale2, gate2 = jnp.split(mod[:, None, :], N_MOD, axis=-1)
        qkv, dz, sb, bg = _in_projection(x, norm1_gain[l], scale1, shift1, _regroup_w_in(w_in[l]))
        o_dn = _deltanet(qkv, dz, bg, dn_conv_w[l], dn_a_log[l], dn_dt_bias[l], dn_out_gain[l])
        qn, kn, vb = _sb_prep(sb, sb_q_gain[l], sb_k_gain[l])
        o_sb = _sb_attention(qn, kn, vb, sb_out_gain[l])
        x1, h2, pq = _out_projection(o_dn, o_sb, x, gate1, w_out[l].astype(BF16), norm2_gain[l], scale2, shift2,
                                     peer_w_query[l].astype(BF16))
        expert, gate = _peer_route(pq.reshape(b * s, -1), peer_sub_keys[l])
        table = jnp.concatenate([peer_w_u[l], peer_w_v[l]], axis=1)
        x = _peer_apply(expert, gate, h2.reshape(b * s, d), x1.reshape(b * s, d), gate2, table, s).reshape(b, s, d)
    return x
```

```python
import functools

import numpy as np
import jax
import jax.numpy as jnp
from jax import lax
from jax.experimental import pallas as pl
from jax.experimental.pallas import tpu as pltpu

F32 = jnp.float32
BF16 = jnp.bfloat16
HIGHEST = lax.Precision.HIGHEST

D_MODEL = 1024
DN_HEAD_DIM = 128
DN_WIDTH = 512
DN_HEADS = 4
DN_CONV = 4
DN_CHUNK = 64
SB_HEAD_DIM = 64
SB_WIDTH = 512
SB_HEADS = 8
PEER_HEADS = 8
PEER_N_KEYS = 128
PEER_HALF = 128
PEER_TOPK = 16
PEER_TOKEN_BLOCK = 128
N_MOD = 6
EPS = 1e-6

LANES = 128
VMEM_LIMIT = 56 * 1024 * 1024

QKV_COLS = 3 * DN_WIDTH
Z_OFF = QKV_COLS
SB_OFF = Z_OFF + DN_WIDTH
BG_OFF = SB_OFF + 3 * SB_WIDTH
IN_COLS_PADDED = BG_OFF + LANES

TS_PROJ = 256
SB_TILE = 256


def _params(*sem):
    return pltpu.CompilerParams(dimension_semantics=sem, vmem_limit_bytes=VMEM_LIMIT)


def _ada_kernel(c_ref, w_ref, b_ref, o_ref):
    c = c_ref[...]
    sc = c * jax.nn.sigmoid(c)
    o_ref[...] = jnp.dot(sc, w_ref[...], precision=HIGHEST, preferred_element_type=F32) + b_ref[...]


def _ada_mod(c, w_ada, b_ada):
    b, d = c.shape
    n = w_ada.shape[1]
    tn = 512
    return pl.pallas_call(
        _ada_kernel,
        grid=(n // tn,),
        in_specs=[pl.BlockSpec((b, d), lambda j: (0, 0)),
                  pl.BlockSpec((d, tn), lambda j: (0, j)),
                  pl.BlockSpec((1, tn), lambda j: (0, j))],
        out_specs=pl.BlockSpec((b, tn), lambda j: (0, j)),
        out_shape=jax.ShapeDtypeStruct((b, n), F32),
        compiler_params=_params("parallel"),
        name="ada_mod",
    )(c, w_ada, b_ada.reshape(1, n))


def _modulated_norm(x, gain, scale, shift):
    y = x * lax.rsqrt(jnp.mean(x * x, axis=-1, keepdims=True) + EPS)
    return (y * gain) * (1.0 + scale) + shift


def _inproj_kernel(x_ref, g_ref, sc_ref, sh_ref, w_ref, qkv_ref, z_ref, sb_ref, bg_ref):
    h = _modulated_norm(x_ref[0], g_ref[...], sc_ref[0], sh_ref[0]).astype(BF16)
    qkv_ref[0] = jnp.dot(h, w_ref[:, 0:Z_OFF], preferred_element_type=F32)
    z_ref[0] = jnp.dot(h, w_ref[:, Z_OFF:SB_OFF], preferred_element_type=F32)
    sb_ref[0] = jnp.dot(h, w_ref[:, SB_OFF:BG_OFF], preferred_element_type=F32)
    bg_ref[0] = jnp.dot(h, w_ref[:, BG_OFF:IN_COLS_PADDED], preferred_element_type=F32)


def _in_projection(x, gain, scale, shift, w):
    b, s, d = x.shape
    ts = TS_PROJ
    tok = lambda n: pl.BlockSpec((1, ts, n), lambda i, j: (i, j, 0))
    per_batch = pl.BlockSpec((1, 1, d), lambda i, j: (i, 0, 0))
    return pl.pallas_call(
        _inproj_kernel,
        grid=(b, s // ts),
        in_specs=[tok(d), pl.BlockSpec((1, d), lambda i, j: (0, 0)), per_batch, per_batch,
                  pl.BlockSpec((d, IN_COLS_PADDED), lambda i, j: (0, 0))],
        out_specs=[tok(QKV_COLS), tok(DN_WIDTH), tok(3 * SB_WIDTH), tok(LANES)],
        out_shape=[jax.ShapeDtypeStruct((b, s, QKV_COLS), F32),
                   jax.ShapeDtypeStruct((b, s, DN_WIDTH), F32),
                   jax.ShapeDtypeStruct((b, s, 3 * SB_WIDTH), F32),
                   jax.ShapeDtypeStruct((b, s, LANES), F32)],
        compiler_params=_params("parallel", "parallel"),
        name="in_projection",
    )(x, gain.reshape(1, d), scale, shift, w)


def _head_mean_square(x, blockdiag):
    xx = x * x
    hi = xx.astype(BF16)
    lo = (xx - hi.astype(F32)).astype(BF16)
    ss = jnp.dot(hi, blockdiag, preferred_element_type=F32) + jnp.dot(lo, blockdiag, preferred_element_type=F32)
    return ss * (1.0 / SB_HEAD_DIM)


def _sb_prep_kernel(sb_ref, bd_ref, gq_ref, gk_ref, q_ref, k_ref, v_ref):
    bd = bd_ref[...]
    q = sb_ref[0, :, 0:SB_WIDTH]
    k = sb_ref[0, :, SB_WIDTH:2 * SB_WIDTH]
    qn = q * lax.rsqrt(_head_mean_square(q, bd) + EPS) * gq_ref[...]
    kn = k * lax.rsqrt(_head_mean_square(k, bd) + EPS) * gk_ref[...]
    q_ref[0] = (qn * SB_HEAD_DIM ** -0.5).astype(BF16)
    k_ref[0] = kn.astype(BF16)
    v_ref[0] = sb_ref[0, :, 2 * SB_WIDTH:3 * SB_WIDTH].astype(BF16)


def _sb_prep(sb, q_gain, k_gain):
    b, s, _ = sb.shape
    ts = TS_PROJ
    head_of = np.arange(SB_WIDTH) // SB_HEAD_DIM
    blockdiag = jnp.asarray(head_of[:, None] == head_of[None, :], BF16)
    tok = lambda n: pl.BlockSpec((1, ts, n), lambda i, j: (i, j, 0))
    const = lambda r, c: pl.BlockSpec((r, c), lambda i, j: (0, 0))
    out = jax.ShapeDtypeStruct((b, s, SB_WIDTH), BF16)
    return pl.pallas_call(
        _sb_prep_kernel,
        grid=(b, s // ts),
        in_specs=[tok(3 * SB_WIDTH), const(SB_WIDTH, SB_WIDTH), const(1, SB_WIDTH), const(1, SB_WIDTH)],
        out_specs=[tok(SB_WIDTH)] * 3,
        out_shape=[out] * 3,
        compiler_params=_params("parallel", "parallel"),
        name="sb_prep",
    )(sb, blockdiag, jnp.tile(q_gain, SB_HEADS).reshape(1, SB_WIDTH), jnp.tile(k_gain, SB_HEADS).reshape(1, SB_WIDTH))


HEADS_PER_STEP = LANES // SB_HEAD_DIM


def _sb_kernel(qi_ref, kj_ref, q_ref, k_ref, v_ref, u_ref, g_ref, o_ref, acc_ref, carry_ref):
    p = pl.program_id(2)
    qi = qi_ref[p]
    kj = kj_ref[p]
    t = SB_TILE

    def step(diagonal):
        for h in range(HEADS_PER_STEP):
            lanes = slice(h * SB_HEAD_DIM, (h + 1) * SB_HEAD_DIM)
            q = q_ref[0, :, lanes]
            k = k_ref[0, :, lanes]
            v = v_ref[0, :, lanes]
            z = lax.dot_general(q, k, (((1,), (1,)), ((), ())), preferred_element_type=F32)
            log_beta = jnp.minimum(z, 0.0) - jnp.log1p(jnp.exp(-jnp.abs(z)))
            log_1m = log_beta - z
            if diagonal:
                causal = lax.broadcasted_iota(jnp.int32, (t, t), 1) < lax.broadcasted_iota(jnp.int32, (t, t), 0)
                log_1m = jnp.where(causal, log_1m, 0.0)
            hi = log_1m.astype(BF16)
            lo = (log_1m - hi.astype(F32)).astype(BF16)
            u = u_ref[...]
            after = jnp.dot(hi, u, preferred_element_type=F32) + jnp.dot(lo, u, preferred_element_type=F32)
            if diagonal:
                a = jnp.where(causal, jnp.exp(log_beta + after), 0.0)
                acc_ref[h] = jnp.dot(a.astype(BF16), v, preferred_element_type=F32)
                carry_ref[h] = jnp.sum(log_1m, axis=-1, keepdims=True)
            else:
                carry = carry_ref[h]
                a = jnp.exp(log_beta + after + carry)
                acc_ref[h] += jnp.dot(a.astype(BF16), v, preferred_element_type=F32)
                carry_ref[h] = carry + jnp.sum(log_1m, axis=-1, keepdims=True)

    @pl.when(kj == qi)
    def _():
        step(True)

    @pl.when(kj < qi)
    def _():
        step(False)

    @pl.when(kj == 0)
    def _():
        for h in range(HEADS_PER_STEP):
            o = acc_ref[h]
            o = o * lax.rsqrt(jnp.mean(o * o, axis=-1, keepdims=True) + EPS) * g_ref[...]
            o_ref[0, :, h * SB_HEAD_DIM:(h + 1) * SB_HEAD_DIM] = o


def _sb_attention(qn, kn, vb, out_gain):
    b, s, _ = qn.shape
    t = SB_TILE
    n = s // t
    idx = np.arange(t)
    suffix = jnp.asarray(idx[:, None] > idx[None, :], BF16)
    pairs = [(qi, kj) for qi in range(n) for kj in range(qi, -1, -1)]
    qi_of = jnp.asarray([p[0] for p in pairs], jnp.int32)
    kj_of = jnp.asarray([p[1] for p in pairs], jnp.int32)
    qspec = pl.BlockSpec((1, t, LANES), lambda bi, hp, p, qi_r, kj_r: (bi, qi_r[p], hp))
    kspec = pl.BlockSpec((1, t, LANES), lambda bi, hp, p, qi_r, kj_r: (bi, kj_r[p], hp))
    return pl.pallas_call(
        _sb_kernel,
        grid_spec=pltpu.PrefetchScalarGridSpec(
            num_scalar_prefetch=2,
            grid=(b, SB_WIDTH // LANES, len(pairs)),
            in_specs=[qspec, kspec, kspec,
                      pl.BlockSpec((t, t), lambda bi, hp, p, qi_r, kj_r: (0, 0)),
                      pl.BlockSpec((1, SB_HEAD_DIM), lambda bi, hp, p, qi_r, kj_r: (0, 0))],
            out_specs=qspec,
            scratch_shapes=[pltpu.VMEM((HEADS_PER_STEP, t, SB_HEAD_DIM), F32),
                            pltpu.VMEM((HEADS_PER_STEP, t, 1), F32)]),
        out_shape=jax.ShapeDtypeStruct((b, s, SB_WIDTH), F32),
        compiler_params=_params("parallel", "parallel", "arbitrary"),
        name="sb_attention",
    )(qi_of, kj_of, qn, kn, vb, suffix, out_gain.reshape(1, SB_HEAD_DIM))


def _outproj_kernel(odn_ref, osb_ref, x_ref, g1_ref, w_ref, n2_ref, sc2_ref, sh2_ref, wq_ref, x1_ref, h2_ref, pq_ref):
    mixed = (jnp.dot(odn_ref[0].astype(BF16), w_ref[0:DN_WIDTH, :], preferred_element_type=F32)
             + jnp.dot(osb_ref[0].astype(BF16), w_ref[DN_WIDTH:DN_WIDTH + SB_WIDTH, :], preferred_element_type=F32))
    x1 = x_ref[0] + g1_ref[0] * mixed
    x1_ref[0] = x1
    h2 = _modulated_norm(x1, n2_ref[...], sc2_ref[0], sh2_ref[0])
    h2_ref[0] = h2
    pq_ref[0] = jnp.dot(h2.astype(BF16), wq_ref[...], preferred_element_type=F32).astype(BF16)


def _out_projection(o_dn, o_sb, x, gate1, w_out, gain2, scale2, shift2, w_query):
    b, s, d = x.shape
    nq = w_query.shape[1]
    ts = TS_PROJ
    tok = lambda n: pl.BlockSpec((1, ts, n), lambda i, j: (i, j, 0))
    per_batch = pl.BlockSpec((1, 1, d), lambda i, j: (i, 0, 0))
    const = lambda r, c: pl.BlockSpec((r, c), lambda i, j: (0, 0))
    out = jax.ShapeDtypeStruct((b, s, d), F32)
    return pl.pallas_call(
        _outproj_kernel,
        grid=(b, s // ts),
        in_specs=[tok(DN_WIDTH), tok(SB_WIDTH), tok(d), per_batch, const(d, d), const(1, d), per_batch, per_batch,
                  const(d, nq)],
        out_specs=[tok(d), tok(d), tok(nq)],
        out_shape=[out, out, jax.ShapeDtypeStruct((b, s, nq), BF16)],
        compiler_params=_params("parallel", "parallel"),
        name="out_projection",
    )(o_dn, o_sb, x, gate1, w_out, gain2.reshape(1, d), scale2, shift2, w_query)


ROUTE_TOKENS = LANES


def _top_k_sublanes(scores, k):
    n, tokens = scores.shape
    row = lax.broadcasted_iota(jnp.int32, scores.shape, 0).astype(F32)
    rank = lax.broadcasted_iota(jnp.int32, (k, tokens), 0)

    def body(i, state):
        s, vals, rows = state
        m = jnp.max(s, axis=0, keepdims=True)
        pos = jnp.min(jnp.where(s == m, row, float(n)), axis=0, keepdims=True)
        hit = rank == i
        return jnp.where(row == pos, -jnp.inf, s), jnp.where(hit, m, vals), jnp.where(hit, pos, rows)

    zeros = jnp.zeros((k, tokens), F32)
    _, vals, rows = lax.fori_loop(0, k, body, (scores, zeros, zeros))
    return vals, rows


def _select_row(table, index):
    out = jnp.zeros_like(table)
    for r in range(table.shape[0]):
        out = out + jnp.where(index == float(r), table[r:r + 1, :], 0.0)
    return out


def _route_kernel(pq_ref, keys_ref, e_ref, g_ref):
    k = PEER_TOPK
    vals, ids = [], []
    for p in range(2):
        q = pq_ref[:, p * PEER_HALF:(p + 1) * PEER_HALF]
        s = lax.dot_general(keys_ref[0, p], q, (((1,), (1,)), ((), ())), preferred_element_type=F32)
        v, i = _top_k_sublanes(s, k)
        vals.append(v)
        ids.append(i)
    cand = jnp.concatenate([vals[0][a:a + 1, :] + vals[1] for a in range(k)], axis=0)
    top_s, top_row = _top_k_sublanes(cand, k)
    a = jnp.floor(top_row * (1.0 / k))
    b = top_row - a * k
    expert = _select_row(ids[0], a) * PEER_N_KEYS + _select_row(ids[1], b)
    ex = jnp.exp(top_s - jnp.max(top_s, axis=0, keepdims=True))
    g_ref[0] = ex / jnp.sum(ex, axis=0, keepdims=True)
    e_ref[0] = expert.astype(jnp.int32)


def _peer_route(pq, sub_keys):
    tkn = pq.shape[0]
    tt = ROUTE_TOKENS
    out = lambda dt: jax.ShapeDtypeStruct((PEER_HEADS, PEER_TOPK, tkn), dt)
    ospec = pl.BlockSpec((1, PEER_TOPK, tt), lambda i, hd: (hd, 0, i))
    expert, gate = pl.pallas_call(
        _route_kernel,
        grid=(tkn // tt, PEER_HEADS),
        in_specs=[pl.BlockSpec((tt, 2 * PEER_HALF), lambda i, hd: (i, hd)),
                  pl.BlockSpec((1, 2, PEER_N_KEYS, PEER_HALF), lambda i, hd: (hd, 0, 0, 0))],
        out_specs=[ospec, ospec],
        out_shape=[out(jnp.int32), out(F32)],
        compiler_params=_params("parallel", "arbitrary"),
        name="peer_route",
    )(pq, sub_keys.astype(BF16))
    flat = lambda a: a.transpose(2, 0, 1).reshape(tkn, PEER_HEADS * PEER_TOPK)
    return flat(expert), flat(gate)


APPLY_TOKENS = 8
ROWS_PER_TOKEN = PEER_HEADS * PEER_TOPK
APPLY_ROWS = APPLY_TOKENS * ROWS_PER_TOKEN
SUBLANES = 8
ROW_GROUPS = ROWS_PER_TOKEN // SUBLANES


def _column_sums_of_8(p):
    sub = lax.broadcasted_iota(jnp.int32, (SUBLANES, LANES), 0)
    lo4 = sub < 4
    b = [jnp.where(lo4, p[j], p[j + 4]) + pltpu.roll(jnp.where(lo4, p[j + 4], p[j]), 4, axis=0) for j in range(4)]
    lo2 = (sub & 2) == 0
    c = [jnp.where(lo2, b[j] + pltpu.roll(b[j], 6, axis=0), b[j + 2] + pltpu.roll(b[j + 2], 2, axis=0))
         for j in range(2)]
    even = (sub & 1) == 0
    return jnp.where(even, c[0] + pltpu.roll(c[0], 7, axis=0), c[1] + pltpu.roll(c[1], 1, axis=0))


def _peer_apply_kernel(idx_ref, idx_next_ref, gate_ref, h_ref, x1_ref, g2_ref, eye_ref, tab_ref, o_ref,
                       buf0_ref, buf1_ref, actb_ref, sem_ref):
    i = pl.program_id(0)
    n = pl.num_programs(0)
    half = ROWS_PER_TOKEN // 2

    def copy(ids_ref, r, dst_ref, sem):
        return pltpu.make_async_copy(tab_ref.at[ids_ref[0, 0, r]], dst_ref.at[r], sem)

    def wait_all(dst_ref, sem):
        pltpu.make_async_copy(tab_ref.at[pl.ds(0, APPLY_ROWS)], dst_ref, sem).wait()

    @pl.when(i == 0)
    def _():
        def body(r, c):
            copy(idx_ref, r, buf0_ref, sem_ref.at[0]).start()
            return c
        lax.fori_loop(0, APPLY_ROWS, body, 0, unroll=8)

    def step(cur_ref, cur_sem, nxt_ref, nxt_sem):
        wait_all(cur_ref, cur_sem)
        gate_t = lax.dot_general(eye_ref[...], gate_ref[...], (((1,), (1,)), ((), ())),
                                 precision=HIGHEST, preferred_element_type=F32)
        tok_lane = lax.broadcasted_iota(jnp.int32, (ROWS_PER_TOKEN, APPLY_TOKENS), 1)
        pre = jnp.zeros((ROWS_PER_TOKEN, APPLY_TOKENS), F32)
        for t in range(APPLY_TOKENS):
            for r in range(t * ROWS_PER_TOKEN, t * ROWS_PER_TOKEN + half):
                copy(idx_next_ref, r, nxt_ref, nxt_sem).start()
            h = h_ref[t]
            cols = []
            for g in range(ROW_GROUPS):
                base = t * ROWS_PER_TOKEN + g * SUBLANES
                prods = [cur_ref[base + r, 0:SUBLANES, :] * h for r in range(SUBLANES)]
                cols.append(jnp.sum(_column_sums_of_8(prods), axis=-1, keepdims=True))
            pre = jnp.where(tok_lane == t, jnp.concatenate(cols, axis=0), pre)
        act = 0.5 * pre * (1.0 + lax.erf(pre * (2.0 ** -0.5))) * gate_t
        for t in range(APPLY_TOKENS):
            for r in range(t * ROWS_PER_TOKEN + half, (t + 1) * ROWS_PER_TOKEN):
                copy(idx_next_ref, r, nxt_ref, nxt_sem).start()
            actb_ref[...] = jnp.broadcast_to(act[:, t:t + 1], (ROWS_PER_TOKEN, LANES))
            parts = [jnp.zeros((SUBLANES, LANES), F32) for _ in range(4)]
            for k in range(ROWS_PER_TOKEN):
                v = cur_ref[t * ROWS_PER_TOKEN + k, SUBLANES:2 * SUBLANES, :]
                parts[k % 4] = parts[k % 4] + actb_ref[k:k + 1, :] * v
            y = (parts[0] + parts[1]) + (parts[2] + parts[3])
            o_ref[t] = x1_ref[t] + g2_ref[0] * y

    @pl.when(i % 2 == 0)
    def _():
        step(buf0_ref, sem_ref.at[0], buf1_ref, sem_ref.at[1])

    @pl.when(i % 2 == 1)
    def _():
        step(buf1_ref, sem_ref.at[1], buf0_ref, sem_ref.at[0])

    @pl.when(jnp.logical_and(i == n - 1, i % 2 == 0))
    def _():
        wait_all(buf1_ref, sem_ref.at[1])

    @pl.when(jnp.logical_and(i == n - 1, i % 2 == 1))
    def _():
        wait_all(buf0_ref, sem_ref.at[0])


def _peer_apply(expert, gate, h2, x1, gate2, table, tokens_per_batch):
    tkn, d = h2.shape
    tt = APPLY_TOKENS
    n = tkn // tt
    steps_per_batch = tokens_per_batch // tt
    fold = d // LANES
    assert fold == SUBLANES
    ids = expert.reshape(n, 1, APPLY_ROWS)
    folded = lambda a: a.reshape(a.shape[0], fold, LANES)
    tok3 = pl.BlockSpec((tt, fold, LANES), lambda i: (i, 0, 0))
    smem = lambda imap: pl.BlockSpec((1, 1, APPLY_ROWS), imap, memory_space=pltpu.SMEM)
    out = pl.pallas_call(
        _peer_apply_kernel,
        grid=(n,),
        in_specs=[smem(lambda i: (i, 0, 0)),
                  smem(lambda i: (jnp.minimum(i + 1, n - 1), 0, 0)),
                  pl.BlockSpec((tt, ROWS_PER_TOKEN), lambda i: (i, 0)), tok3, tok3,
                  pl.BlockSpec((1, fold, LANES), lambda i: (i // steps_per_batch, 0, 0)),
                  pl.BlockSpec((ROWS_PER_TOKEN, ROWS_PER_TOKEN), lambda i: (0, 0)),
                  pl.BlockSpec(memory_space=pl.ANY)],
        out_specs=tok3,
        out_shape=jax.ShapeDtypeStruct((tkn, fold, LANES), F32),
        scratch_shapes=[pltpu.VMEM((APPLY_ROWS, 2 * fold, LANES), F32),
                        pltpu.VMEM((APPLY_ROWS, 2 * fold, LANES), F32),
                        pltpu.VMEM((ROWS_PER_TOKEN, LANES), F32),
                        pltpu.SemaphoreType.DMA((2,))],
        compiler_params=_params("arbitrary"),
        name="peer_apply",
    )(ids, ids, gate, folded(h2), folded(x1), folded(gate2.reshape(-1, d)), jnp.eye(ROWS_PER_TOKEN, dtype=F32),
      table.reshape(table.shape[0], 2 * fold, LANES))
    return out.reshape(tkn, d)


DN_TOKENS = 512
CONV_HALO = 8
BETA_OFF, G_OFF = 0, DN_HEADS


def _dn_prep_kernel(x_ref, prev_ref, bg_ref, cw_ref, alog_ref, dtb_ref, q_ref, k_ref, v_ref, gb_ref):
    j = pl.program_id(1)
    x = x_ref[0]
    ts = x.shape[0]
    prev = jnp.where(j > 0, prev_ref[0], 0.0)
    ext = jnp.concatenate([prev, x], axis=0)
    acc = x * cw_ref[DN_CONV - 1:DN_CONV, :]
    for back in range(1, DN_CONV):
        shifted = pltpu.roll(ext, back, axis=0)[CONV_HALO:CONV_HALO + ts, :]
        acc = acc + shifted * cw_ref[DN_CONV - 1 - back:DN_CONV - back, :]
    y = acc * jax.nn.sigmoid(acc)
    for h in range(DN_HEADS):
        lanes = slice(h * DN_HEAD_DIM, (h + 1) * DN_HEAD_DIM)
        q = y[:, lanes]
        k = y[:, DN_WIDTH + h * DN_HEAD_DIM:DN_WIDTH + (h + 1) * DN_HEAD_DIM]
        q_ref[0, :, lanes] = q * lax.rsqrt(jnp.sum(q * q, axis=-1, keepdims=True) + EPS) * DN_HEAD_DIM ** -0.5
        k_ref[0, :, lanes] = k * lax.rsqrt(jnp.sum(k * k, axis=-1, keepdims=True) + EPS)
    v_ref[0] = y[:, 2 * DN_WIDTH:3 * DN_WIDTH]
    bg = bg_ref[0]
    sp_in = bg + dtb_ref[...]
    g = -jnp.exp(alog_ref[...]) * (jnp.maximum(sp_in, 0.0) + jnp.log1p(jnp.exp(-jnp.abs(sp_in))))
    lane = lax.broadcasted_iota(jnp.int32, bg.shape, 1)
    gb_ref[0] = jnp.where(lane < DN_HEADS, jax.nn.sigmoid(bg), g)


def _dn_prep(qkv, bg, conv_w, a_log, dt_bias):
    b, s, w = qkv.shape
    ts = DN_TOKENS
    tok = lambda n: pl.BlockSpec((1, ts, n), lambda i, j: (i, j, 0))
    halo = pl.BlockSpec((1, CONV_HALO, w), lambda i, j: (i, jnp.maximum(j * (ts // CONV_HALO) - 1, 0), 0))
    const = lambda r, c: pl.BlockSpec((r, c), lambda i, j: (0, 0))
    on_decay_lanes = lambda p: jnp.zeros((1, LANES), F32).at[0, DN_HEADS:2 * DN_HEADS].set(p)
    wide = jax.ShapeDtypeStruct((b, s, DN_WIDTH), F32)
    return pl.pallas_call(
        _dn_prep_kernel,
        grid=(b, s // ts),
        in_specs=[tok(w), halo, tok(LANES), const(DN_CONV, w), const(1, LANES), const(1, LANES)],
        out_specs=[tok(DN_WIDTH)] * 3 + [tok(LANES)],
        out_shape=[wide, wide, wide, jax.ShapeDtypeStruct((b, s, LANES), F32)],
        compiler_params=_params("parallel", "parallel"),
        name="dn_prep",
    )(qkv, qkv, bg, conv_w, on_decay_lanes(a_log), on_decay_lanes(dt_bias))


def _dot_hi(a, b):
    return jnp.dot(a, b, precision=HIGHEST, preferred_element_type=F32)


def _dn_core_kernel(q_ref, k_ref, v_ref, gb_ref, z_ref, gain_ref, o_ref, state_ref):
    c = DN_CHUNK

    @pl.when(pl.program_id(1) == 0)
    def _():
        state_ref[...] = jnp.zeros_like(state_ref)

    row = lax.broadcasted_iota(jnp.int32, (c, c), 0)
    col = lax.broadcasted_iota(jnp.int32, (c, c), 1)
    lower_incl = row >= col
    lower_strict = row > col
    eye = jnp.where(row == col, 1.0, 0.0)
    cumsum_mat = jnp.where(lower_incl, 1.0, 0.0)
    first_lane = jnp.where(lax.broadcasted_iota(jnp.int32, (c, LANES), 1) == 0, 1.0, 0.0)

    def chunk(ci, carry):
        rows = pl.ds(pl.multiple_of(ci * c, c), c)
        for h in range(DN_HEADS):
            lanes = slice(h * DN_HEAD_DIM, (h + 1) * DN_HEAD_DIM)
            q = q_ref[0, rows, lanes]
            k = k_ref[0, rows, lanes]
            v = v_ref[0, rows, lanes]
            g_col = jnp.broadcast_to(gb_ref[0, rows, G_OFF + h:G_OFF + h + 1], (c, LANES))
            beta = jnp.broadcast_to(gb_ref[0, rows, BETA_OFF + h:BETA_OFF + h + 1], (c, LANES))
            gc = _dot_hi(cumsum_mat, g_col)
            gc_row = lax.dot_general(first_lane, gc, (((1,), (1,)), ((), ())), precision=HIGHEST,
                                     preferred_element_type=F32)
            decay = jnp.where(lower_incl, jnp.exp(jnp.where(lower_incl, gc[:, :c] - gc_row, 0.0)), 0.0)
            kbeta = k * beta
            kk = lax.dot_general(kbeta, k, (((1,), (1,)), ((), ())), preferred_element_type=F32)
            n_pow = -jnp.where(lower_strict, kk * decay, 0.0)
            inv = eye + n_pow
            for _ in range(5):
                n_pow = _dot_hi(n_pow, n_pow)
                inv = inv + _dot_hi(inv, n_pow)
            e_gc = jnp.exp(gc)
            w_c = _dot_hi(inv, kbeta * e_gc)
            u_c = _dot_hi(inv, v * beta)
            attn = lax.dot_general(q, k, (((1,), (1,)), ((), ())), preferred_element_type=F32) * decay
            gc_last = gc[c - 1:c, :]
            state = state_ref[h]
            v_new = u_c - jnp.dot(w_c, state, preferred_element_type=F32)
            o = jnp.dot(q * e_gc, state, preferred_element_type=F32) + jnp.dot(attn, v_new, preferred_element_type=F32)
            k_dec = k * jnp.exp(gc_last - gc)
            state_ref[h] = state * jnp.exp(gc_last) + lax.dot_general(
                k_dec, v_new, (((0,), (0,)), ((), ())), preferred_element_type=F32)
            z = z_ref[0, rows, lanes]
            o = o * lax.rsqrt(jnp.mean(o * o, axis=-1, keepdims=True) + EPS) * gain_ref[...]
            o_ref[0, rows, lanes] = o * (z * jax.nn.sigmoid(z))
        return carry

    lax.fori_loop(0, DN_TOKENS // c, chunk, 0)


def _dn_core(q, k, v, gb, dz, out_gain):
    b, s, w = q.shape
    ts = DN_TOKENS
    tok = lambda n: pl.BlockSpec((1, ts, n), lambda i, j: (i, j, 0))
    return pl.pallas_call(
        _dn_core_kernel,
        grid=(b, s // ts),
        in_specs=[tok(w), tok(w), tok(w), tok(LANES), tok(w), pl.BlockSpec((1, DN_HEAD_DIM), lambda i, j: (0, 0))],
        out_specs=tok(w),
        out_shape=jax.ShapeDtypeStruct((b, s, w), F32),
        scratch_shapes=[pltpu.VMEM((DN_HEADS, DN_HEAD_DIM, DN_HEAD_DIM), F32)],
        compiler_params=_params("parallel", "arbitrary"),
        name="dn_core",
    )(q, k, v, gb, dz, out_gain.reshape(1, DN_HEAD_DIM))


def _deltanet(qkv, dz, bg, conv_w, a_log, dt_bias, out_gain):
    q, k, v, gb = _dn_prep(qkv, bg, conv_w, a_log, dt_bias)
    return _dn_core(q, k, v, gb, dz, out_gain)


def _regroup_w_in(w_in):
    bg0 = 4 * DN_WIDTH
    sb0 = bg0 + 2 * DN_HEADS
    pad = jnp.zeros((w_in.shape[0], LANES - 2 * DN_HEADS), w_in.dtype)
    return jnp.concatenate([w_in[:, :bg0], w_in[:, sb0:], w_in[:, bg0:sb0], pad], axis=1).astype(BF16)


def kernel(x, c, w_ada, b_ada, norm1_gain, w_in, dn_conv_w, dn_a_log, dn_dt_bias, dn_out_gain, sb_q_gain,
           sb_k_gain, sb_out_gain, w_out, norm2_gain, peer_w_query, peer_sub_keys, peer_w_u, peer_w_v):
    b, s, d = x.shape
    for l in range(w_ada.shape[0]):
        mod = _ada_mod(c, w_ada[l], b_ada[l])
        shift1, scale1, gate1, shift2, scale2, gate2 = jnp.split(mod[:, None, :], N_MOD, axis=-1)
        qkv, dz, sb, bg = _in_projection(x, norm1_gain[l], scale1, shift1, _regroup_w_in(w_in[l]))
        o_dn = _deltanet(qkv, dz, bg, dn_conv_w[l], dn_a_log[l], dn_dt_bias[l], dn_out_gain[l])
        qn, kn, vb = _sb_prep(sb, sb_q_gain[l], sb_k_gain[l])
        o_sb = _sb_attention(qn, kn, vb, sb_out_gain[l])
        x1, h2, pq = _out_projection(o_dn, o_sb, x, gate1, w_out[l].astype(BF16), norm2_gain[l], scale2, shift2,
                                     peer_w_query[l].astype(BF16))
        expert, gate = _peer_route(pq.reshape(b * s, -1), peer_sub_keys[l])
        table = jnp.concatenate([peer_w_u[l], peer_w_v[l]], axis=1)
        x = _peer_apply(expert, gate, h2.reshape(b * s, d), x1.reshape(b * s, d), gate2, table, s).reshape(b, s, d)
    return x
```
